```python
import math
import jax
import jax.numpy as jnp
from jax import lax
import numpy as np

D_MODEL = 1024
BATCH = 4
SEQ = 4096
DEPTH = 4
DEC_BATCH = 128
DEC_SEQ = 8
PAST_LEN = 2048
PAGE_SIZE = 128

MIX_WIDTH = D_MODEL
BRANCH_WIDTH = MIX_WIDTH // 4
A_WIDTH = BRANCH_WIDTH
A_HEADS = 4
A_HEAD_DIM = A_WIDTH // A_HEADS
A_PATTERNS = ((128, 1), (512, 4), (2048, 16))
A_WINDOW_MAX = 2048
A_BLOCK = 128
B_WIDTH = BRANCH_WIDTH
B_HEADS = 4
B_DK = B_WIDTH // B_HEADS
B_DV = B_WIDTH // B_HEADS
CONV_W = 4
DELTA_CHUNK = 64
C_WIDTH = BRANCH_WIDTH
C_HEADS = 4
C_DK = C_WIDTH // C_HEADS
C_DV = C_WIDTH // C_HEADS
HGRN_CHUNK = 16
D_WIDTH = MIX_WIDTH - A_WIDTH - B_WIDTH - C_WIDTH
D_GROUPS = 4
D_GROUP_W = D_WIDTH // D_GROUPS
POOL_WINDOWS = (2, 4, 8, 16)
POOL_MAX = 16
NORM_EPS = 1e-6
NEG_MASK = -1e30
IN_SPLITS = (A_WIDTH, A_WIDTH, A_WIDTH, A_WIDTH,
             3 * B_WIDTH, B_WIDTH, B_HEADS, B_HEADS,
             C_WIDTH, C_WIDTH, C_WIDTH, C_WIDTH,
             D_WIDTH, D_WIDTH)
IN_WIDTH = sum(IN_SPLITS)
F32 = jnp.float32

kernel_name = 'hybrid_dilated_delta_hgrn2_pool_decode_step'


def _rms(x, w):
    xf = x.astype(F32)
    y = xf * lax.rsqrt(jnp.mean(xf * xf, axis=-1, keepdims=True) + NORM_EPS)
    return y * w.astype(F32)


def _l2norm(x):
    return x * lax.rsqrt(jnp.sum(x * x, axis=-1, keepdims=True) + 1e-6)


def _alibi_slopes():
    return 2.0 ** (-8.0 * (jnp.arange(A_HEADS, dtype=F32) + 1.0) / A_HEADS)


def _dilated_band(q, k, v, slopes, dil, steps):
    b, s, h, dh = q.shape
    n = s // dil

    def to_res(t):
        return t.reshape(b, n, dil, h, dh).transpose(0, 2, 1, 3, 4).reshape(b * dil, n, h, dh)

    qr, kr, vr = to_res(q), to_res(k), to_res(v)
    bq = math.gcd(n, A_BLOCK)
    nb = n // bq
    nk = bq + steps
    pad = ((0, 0), (steps, 0), (0, 0), (0, 0))
    idx = jnp.arange(nb)[:, None] * bq + jnp.arange(nk)[None, :]
    kb = jnp.pad(kr, pad)[:, idx]
    vb = jnp.pad(vr, pad)[:, idx]
    qb = qr.reshape(b * dil, nb, bq, h, dh)
    sc = jnp.einsum('znqhd,znkhd->znhqk', qb, kb)
    dist = (jnp.arange(bq)[:, None] + steps) - jnp.arange(nk)[None, :]
    valid = (dist >= 0) & (dist <= steps) & ((idx - steps)[:, None, :] >= 0)
    sc = sc - slopes[:, None, None] * (dist * dil).astype(F32)
    sc = jnp.where(valid[None, :, None], sc, NEG_MASK)
    m = sc.max(-1)
    p = jnp.exp(sc - m[..., None])
    den = p.sum(-1)
    num = jnp.einsum('znhqk,znkhd->znqhd', p, vb)

    def back(t):
        rest = t.shape[3:]
        t = t.reshape((b, dil, n) + rest)
        return jnp.swapaxes(t, 1, 2).reshape((b, s) + rest)

    return back(jnp.moveaxis(m, 2, 3)), back(jnp.moveaxis(den, 2, 3)), back(num)


def _dilated_gather(q, k_all, v_all, slopes, dil, steps, buf):
    t = q.shape[1]
    j = jnp.arange(steps + 1)
    c = buf + jnp.arange(t)[:, None] - j[None, :] * dil
    valid = c >= 0
    cc = jnp.maximum(c, 0)
    kg = k_all[:, cc]
    vg = v_all[:, cc]
    sc = jnp.einsum('bthd,btjhd->bthj', q, kg) - slopes[:, None] * (j * dil).astype(F32)[None, :]
    sc = jnp.where(valid[None, :, None, :], sc, NEG_MASK)
    m = sc.max(-1)
    p = jnp.exp(sc - m[..., None])
    den = p.sum(-1)
    num = jnp.einsum('bthj,btjhd->bthd', p, vg)
    return m, den, num


def _mixer_attn(q, k, v, gate, kv_cache):
    b, l, _ = q.shape

    def heads(t):
        return t.reshape(b, l, A_HEADS, A_HEAD_DIM)

    k_h, v_h = heads(k), heads(v)
    qf = heads(q).astype(F32) * (A_HEAD_DIM ** -0.5)
    kf, vf = k_h.astype(F32), v_h.astype(F32)
    slopes = _alibi_slopes()
    parts = []
    if kv_cache is None:
        for win, dil in A_PATTERNS:
            parts.append(_dilated_band(qf, kf, vf, slopes, dil, win // dil))
        keep = min(A_WINDOW_MAX, l)
        k_rows, v_rows = k_h[:, l - keep:], v_h[:, l - keep:]
    else:
        k_cache, v_cache = kv_cache
        buf = k_cache.shape[1]
        k_all = jnp.concatenate([k_cache.astype(F32), kf], axis=1)
        v_all = jnp.concatenate([v_cache.astype(F32), vf], axis=1)
        for win, dil in A_PATTERNS:
            parts.append(_dilated_gather(qf, k_all, v_all, slopes, dil, win // dil, buf))
        k_rows, v_rows = k_h, v_h
    m = jnp.stack([pt[0] for pt in parts])
    den = jnp.stack([pt[1] for pt in parts])
    num = jnp.stack([pt[2] for pt in parts])
    wgt = jnp.exp(m - m.max(0, keepdims=True))
    o = (wgt[..., None] * num).sum(0) / (wgt * den).sum(0)[..., None]
    y = o.reshape(b, l, A_WIDTH) * jax.nn.silu(gate.astype(F32))
    return y.astype(q.dtype), k_rows, v_rows


def _gated_delta_chunked(q, k, v, g, beta, s0):
    b, h, l, dk = q.shape
    dv = v.shape[-1]
    c = math.gcd(l, DELTA_CHUNK)
    n = l // c

    def rs(t):
        return t.reshape((b, h, n, c) + t.shape[3:])

    q, k, v, g, beta = rs(q), rs(k), rs(v), rs(g), rs(beta)
    gc = jnp.cumsum(g, axis=-1)
    tri = jnp.tril(jnp.ones((c, c), bool))
    decay = jnp.exp(jnp.where(tri, gc[..., :, None] - gc[..., None, :], NEG_MASK))
    kb = k * beta[..., None]
    vb = v * beta[..., None]
    a_mat = jnp.tril(jnp.einsum('bhnid,bhnjd->bhnij', kb, k) * decay, -1)
    t_mat = a_mat + jnp.eye(c, dtype=F32)
    u = lax.linalg.triangular_solve(t_mat, vb, left_side=True, lower=True, unit_diagonal=True)
    w = lax.linalg.triangular_solve(t_mat, kb * jnp.exp(gc)[..., None], left_side=True, lower=True,
                                    unit_diagonal=True)
    qk = jnp.einsum('bhnid,bhnjd->bhnij', q, k) * decay

    def step(s, xs):
        q_n, k_n, u_n, w_n, gc_n, qk_n = xs
        v_new = u_n - jnp.einsum('bhcd,bhde->bhce', w_n, s)
        o = (jnp.einsum('bhcd,bhde->bhce', q_n * jnp.exp(gc_n)[..., None], s)
             + jnp.einsum('bhij,bhje->bhie', qk_n, v_new))
        g_last = gc_n[..., -1]
        s = (s * jnp.exp(g_last)[..., None, None]
             + jnp.einsum('bhcd,bhce->bhde', k_n * jnp.exp(g_last[..., None] - gc_n)[..., None], v_new))
        return s, o

    xs = (q, k, u, w, gc, qk)
    xs = [jnp.moveaxis(t, 2, 0) for t in xs]
    s_fin, o = lax.scan(step, s0, xs)
    o = jnp.moveaxis(o, 0, 2).reshape(b, h, l, dv)
    return o, s_fin


def _mixer_delta(qkv, gate, beta_raw, a_raw, conv_prefix, s0, conv_w, a_log, dt_bias, norm_w):
    b, l, _ = qkv.shape
    xc = jnp.concatenate([conv_prefix.astype(qkv.dtype), qkv], axis=1)
    conv_new = xc[:, xc.shape[1] - (CONV_W - 1):]
    xcf = xc.astype(F32)
    cw = conv_w.astype(F32)
    conv = sum(xcf[:, tap:tap + l] * cw[tap] for tap in range(CONV_W))
    conv = jax.nn.silu(conv)
    q, k, v = jnp.split(conv, 3, axis=-1)

    def heads(t):
        return t.reshape(b, l, B_HEADS, -1).transpose(0, 2, 1, 3)

    q = _l2norm(heads(q)) * (B_DK ** -0.5)
    k = _l2norm(heads(k))
    v = heads(v)
    beta = jax.nn.sigmoid(beta_raw.astype(F32)).transpose(0, 2, 1)
    g = (-jnp.exp(a_log.astype(F32))
         * jax.nn.softplus(a_raw.astype(F32) + dt_bias.astype(F32))).transpose(0, 2, 1)
    o, s_new = _gated_delta_chunked(q, k, v, g, beta, s0.astype(F32))
    o = _rms(o.transpose(0, 2, 1, 3), norm_w).reshape(b, l, B_WIDTH)
    y = o * jax.nn.silu(gate.astype(F32))
    return y.astype(qkv.dtype), conv_new, s_new


def _hgrn_lower_bounds(raw):
    s = jax.nn.softmax(raw.astype(F32), axis=0)
    return jnp.cumsum(s, axis=0) - s[0:1]


def _gla_chunked(q, k, log_f, v, s0):
    b, h, l, dk = q.shape
    dv = v.shape[-1]
    c = math.gcd(l, HGRN_CHUNK)
    n = l // c

    def rs(t):
        return t.reshape((b, h, n, c) + t.shape[3:])

    q, k, log_f, v = rs(q), rs(k), rs(log_f), rs(v)
    gcum = jnp.cumsum(log_f, axis=3)
    tri = jnp.tril(jnp.ones((c, c), bool))
    diff = gcum[:, :, :, :, None, :] - gcum[:, :, :, None, :, :]
    decay = jnp.exp(jnp.where(tri[:, :, None], diff, NEG_MASK))
    a_mat = jnp.einsum('bhnic,bhnijc,bhnjc->bhnij', q, decay, k)
    intra = jnp.einsum('bhnij,bhnje->bhnie', a_mat, v)
    g_last = gcum[:, :, :, -1]
    qg = q * jnp.exp(gcum)
    kg = k * jnp.exp(g_last[:, :, :, None] - gcum)

    def step(s, xs):
        qg_n, kg_n, v_n, intra_n, gl_n = xs
        o = jnp.einsum('bhcd,bhde->bhce', qg_n, s) + intra_n
        s = s * jnp.exp(gl_n)[..., None] + jnp.einsum('bhcd,bhce->bhde', kg_n, v_n)
        return s, o

    xs = [jnp.moveaxis(t, 2, 0) for t in (qg, kg, v, intra, g_last)]
    s_fin, o = lax.scan(step, s0, xs)
    o = jnp.moveaxis(o, 0, 2).reshape(b, h, l, dv)
    return o, s_fin


def _mixer_hgrn(q, f_raw, i, gate, lb, s0, norm_w):
    b, l, _ = q.shape

    def heads(t):
        return t.astype(F32).reshape(b, l, C_HEADS, -1).transpose(0, 2, 1, 3)

    qh = jax.nn.silu(heads(q))
    fr = heads(f_raw)
    lbh = lb.reshape(C_HEADS, 1, C_DK)
    f = lbh + (1.0 - lbh) * jax.nn.sigmoid(fr)
    log_f = jnp.log(f)
    kh = (1.0 - lbh) * jax.nn.sigmoid(-fr)
    vh = heads(i)
    o, s_new = _gla_chunked(qh, kh, log_f, vh, s0.astype(F32))
    o = _rms(o.transpose(0, 2, 1, 3), norm_w).reshape(b, l, C_WIDTH)
    y = o * jax.nn.silu(gate.astype(F32))
    return y.astype(q.dtype), s_new


def _mixer_pool(xd, gate, prefix, start_pos, pool_w, pool_scale):
    b, l, _ = xd.shape
    p_len = POOL_MAX - 1
    xc = jnp.concatenate([prefix.astype(xd.dtype), xd], axis=1)
    pool_new = xc[:, xc.shape[1] - p_len:]
    xcf = xc.astype(F32)
    cs = jnp.concatenate([jnp.zeros((b, 1, D_WIDTH), F32), jnp.cumsum(xcf, axis=1)], axis=1)
    pos = start_pos + jnp.arange(l)
    means = []
    for gi, win in enumerate(POOL_WINDOWS):
        lo, hi = gi * D_GROUP_W, (gi + 1) * D_GROUP_W
        tot = cs[:, p_len + 1:p_len + 1 + l, lo:hi] - cs[:, p_len + 1 - win:p_len + 1 - win + l, lo:hi]
        cnt = jnp.minimum(pos + 1, win).astype(F32)
        means.append(tot / cnt[None, :, None])
    pooled = (jnp.concatenate(means, axis=-1) - xcf[:, p_len:]).reshape(b, l, D_GROUPS, D_GROUP_W)
    y = jnp.einsum('blgc,gcd->blgd', pooled, pool_w.astype(F32)).reshape(b, l, D_WIDTH)
    y = y * pool_scale.astype(F32) * jax.nn.silu(gate.astype(F32))
    return y.astype(xd.dtype), pool_new


def _trunk(x, start_pos, states, norm_w, w_in, w_out, delta_conv_w, delta_a_log, delta_dt_bias,
           delta_norm_w, hgrn_lb_raw, hgrn_norm_w, pool_w, pool_scale, final_norm_w):
    b = x.shape[0]
    dt = x.dtype
    split_pts = [int(s) for s in np.cumsum(IN_SPLITS)[:-1]]
    lower_bounds = _hgrn_lower_bounds(hgrn_lb_raw)
    ks, vs, ds, dcs, hs, ps = [], [], [], [], [], []
    for layer in range(DEPTH):
        if states is None:
            kv = None
            conv_prefix = jnp.zeros((b, CONV_W - 1, 3 * B_WIDTH), dt)
            s_delta = jnp.zeros((b, B_HEADS, B_DK, B_DV), F32)
            s_hgrn = jnp.zeros((b, C_HEADS, C_DK, C_DV), F32)
            pool_prefix = jnp.zeros((b, POOL_MAX - 1, D_WIDTH), dt)
        else:
            c_k, c_v, c_delta, c_conv, c_hgrn, c_pool = states
            kv = (c_k[layer], c_v[layer])
            conv_prefix = c_conv[layer]
            s_delta = c_delta[layer]
            s_hgrn = c_hgrn[layer]
            pool_prefix = c_pool[layer]
        h = _rms(x, norm_w[layer]).astype(dt)
        p = h @ w_in[layer]
        (a_q, a_k, a_v, a_g, b_qkv, b_g, b_beta, b_a,
         c_q, c_f, c_i, c_g, d_x, d_g) = jnp.split(p, split_pts, axis=-1)
        y_a, k_rows, v_rows = _mixer_attn(a_q, a_k, a_v, a_g, kv)
        y_b, conv_new, sd_new = _mixer_delta(b_qkv, b_g, b_beta, b_a, conv_prefix, s_delta,
                                             delta_conv_w[layer], delta_a_log[layer],
                                             delta_dt_bias[layer], delta_norm_w[layer])
        y_c, sh_new = _mixer_hgrn(c_q, c_f, c_i, c_g, lower_bounds[layer], s_hgrn, hgrn_norm_w[layer])
        y_d, pool_new = _mixer_pool(d_x, d_g, pool_prefix, start_pos, pool_w[layer], pool_scale[layer])
        y = jnp.concatenate([y_a, y_b, y_c, y_d], axis=-1) @ w_out[layer]
        x = x + y
        ks.append(k_rows)
        vs.append(v_rows)
        ds.append(sd_new)
        dcs.append(conv_new)
        hs.append(sh_new)
        ps.append(pool_new)
    out = _rms(x, final_norm_w).astype(dt)
    return out, (jnp.stack(ks), jnp.stack(vs), jnp.stack(ds), jnp.stack(dcs), jnp.stack(hs), jnp.stack(ps))


def setup_inputs(seed: int = 0) -> dict:
    key = jax.random.key(seed)
    ks = jax.random.split(key, 24)
    a_buf = min(A_WINDOW_MAX, PAST_LEN)

    def nrm(k, shape, scale):
        return scale * jax.random.normal(k, shape, F32)

    dt_init = jnp.exp(jax.random.uniform(ks[13], (DEPTH, B_HEADS), F32, math.log(1e-3), math.log(1e-1)))
    return {
        'x_prompt': nrm(ks[0], (BATCH, SEQ, D_MODEL), 1.0),
        'x_sample': nrm(ks[1], (DEC_BATCH, DEC_SEQ, D_MODEL), 1.0),
        'cache_attn_k': nrm(ks[2], (DEPTH, DEC_BATCH, a_buf, A_HEADS, A_HEAD_DIM), 1.0),
        'cache_attn_v': nrm(ks[3], (DEPTH, DEC_BATCH, a_buf, A_HEADS, A_HEAD_DIM), 1.0),
        'state_delta': nrm(ks[4], (DEPTH, DEC_BATCH, B_HEADS, B_DK, B_DV), 0.1),
        'state_delta_conv': nrm(ks[5], (DEPTH, DEC_BATCH, CONV_W - 1, 3 * B_WIDTH), 1.0),
        'state_hgrn': nrm(ks[6], (DEPTH, DEC_BATCH, C_HEADS, C_DK, C_DV), 0.3),
        'state_pool': nrm(ks[7], (DEPTH, DEC_BATCH, POOL_MAX - 1, D_WIDTH), 1.0),
        'norm_w': 1.0 + nrm(ks[8], (DEPTH, D_MODEL), 0.1),
        'w_in': nrm(ks[9], (DEPTH, D_MODEL, IN_WIDTH), D_MODEL ** -0.5),
        'w_out': nrm(ks[10], (DEPTH, MIX_WIDTH, D_MODEL), MIX_WIDTH ** -0.5),
        'delta_conv_w': nrm(ks[11], (DEPTH, CONV_W, 3 * B_WIDTH), CONV_W ** -0.5),
        'delta_a_log': jnp.log(jax.random.uniform(ks[12], (DEPTH, B_HEADS), F32, 1.0, 16.0)),
        'delta_dt_bias': dt_init + jnp.log(-jnp.expm1(-dt_init)),
        'delta_norm_w': 1.0 + nrm(ks[14], (DEPTH, B_DV), 0.1),
        'hgrn_lb_raw': 1.0 + nrm(ks[15], (DEPTH, C_WIDTH), 0.1),
        'hgrn_norm_w': 1.0 + nrm(ks[16], (DEPTH, C_DV), 0.1),
        'pool_w': nrm(ks[17], (DEPTH, D_GROUPS, D_GROUP_W, D_GROUP_W), D_GROUP_W ** -0.5),
        'pool_scale': 1.0 + nrm(ks[18], (DEPTH, D_WIDTH), 0.1),
        'final_norm_w': 1.0 + nrm(ks[19], (D_MODEL,), 0.1),
    }


def reference(x_prompt, x_sample, cache_attn_k, cache_attn_v, state_delta, state_delta_conv, state_hgrn,
              state_pool, norm_w, w_in, w_out, delta_conv_w, delta_a_log, delta_dt_bias, delta_norm_w,
              hgrn_lb_raw, hgrn_norm_w, pool_w, pool_scale, final_norm_w):
    y_prompt, st_p = _trunk(x_prompt, 0, None, norm_w, w_in, w_out, delta_conv_w, delta_a_log,
                            delta_dt_bias, delta_norm_w, hgrn_lb_raw, hgrn_norm_w, pool_w, pool_scale,
                            final_norm_w)
    y_sample, st_s = _trunk(x_sample, PAST_LEN,
                            (cache_attn_k, cache_attn_v, state_delta, state_delta_conv, state_hgrn, state_pool),
                            norm_w, w_in, w_out, delta_conv_w, delta_a_log, delta_dt_bias, delta_norm_w,
                            hgrn_lb_raw, hgrn_norm_w, pool_w, pool_scale, final_norm_w)
    k_p, v_p, d_p, dc_p, h_p, pl_p = st_p
    k_s, v_s, d_s, dc_s, h_s, pl_s = st_s
    return (y_prompt, y_sample, k_p, k_s, v_p, v_s, d_p, d_s, dc_p, dc_s, h_p, h_s, pl_p, pl_s)
```

```python
import functools
import math

import jax
import jax.numpy as jnp
from jax import lax
from jax.experimental import pallas as pl
from jax.experimental.pallas import tpu as pltpu

F32 = jnp.float32
BF16 = jnp.bfloat16
HIGHEST = lax.Precision.HIGHEST

N_HEADS = 4
HEAD_DIM = 64
BRANCH = N_HEADS * HEAD_DIM
ATTN_PATTERNS = ((128, 1), (512, 4), (2048, 16))
ATTN_STEPS = 128
CONV_TAPS = 4
DELTA_CHUNK = 64
HGRN_CHUNK = 16
POOL_WINDOWS = (2, 4, 8, 16)
POOL_MAX = 16
PAST_LEN = 2048
NORM_EPS = 1e-6
NEG_MASK = -1e30
ALIBI_SLOPES = tuple(2.0 ** (-8.0 * (h + 1) / N_HEADS) for h in range(N_HEADS))

SUBLANES = 8
LANES = 128
VMEM_LIMIT = 56 * 1024 * 1024

SEG_A = (0, 4 * BRANCH)
SEG_B = (4 * BRANCH, 8 * BRANCH)
SEG_C = (8 * BRANCH, 12 * BRANCH)
SEG_D = (12 * BRANCH, 14 * BRANCH)
SEG_E = (14 * BRANCH, 14 * BRANCH + LANES)
IN_PAD = SEG_E[1]


def _sigmoid(x):
    return 1.0 / (1.0 + jnp.exp(-x))


def _silu(x):
    return x * _sigmoid(x)


def _softplus(x):
    return jnp.maximum(x, 0.0) + jnp.log(1.0 + jnp.exp(-jnp.abs(x)))


def _head_ones():
    r = lax.broadcasted_iota(jnp.int32, (BRANCH, BRANCH), 0) // HEAD_DIM
    c = lax.broadcasted_iota(jnp.int32, (BRANCH, BRANCH), 1) // HEAD_DIM
    return jnp.where(r == c, 1.0, 0.0).astype(BF16)


def _head_sum(x, g):
    hi = x.astype(BF16)
    lo = (x - hi.astype(F32)).astype(BF16)
    return jnp.dot(hi, g, preferred_element_type=F32) + jnp.dot(lo, g, preferred_element_type=F32)


def _bmm(a, b):
    return jnp.einsum('bij,bjk->bik', a, b, preferred_element_type=F32)


def _bmm_nt(a, b):
    return jnp.einsum('bid,bjd->bij', a, b, preferred_element_type=F32)


def _bmm_tn(a, b):
    return jnp.einsum('bci,bcj->bij', a, b, preferred_element_type=F32)


def _bmm_hi(a, b):
    return jnp.einsum('bij,bjk->bik', a, b, preferred_element_type=F32, precision=HIGHEST)


def _inproj_body(x_ref, nw_ref, w_ref, pa_ref, pb_ref, pc_ref, pd_ref, pe_ref):
    x = x_ref[...]
    ms = jnp.mean(x * x, axis=-1, keepdims=True)
    h = (x * lax.rsqrt(ms + NORM_EPS) * nw_ref[...]).astype(BF16)
    for ref, (lo, hi) in ((pa_ref, SEG_A), (pb_ref, SEG_B), (pc_ref, SEG_C), (pd_ref, SEG_D), (pe_ref, SEG_E)):
        ref[...] = jnp.dot(h, w_ref[:, lo:hi], preferred_element_type=F32)


def _inproj(x2d, nw, w, tm):
    t, d = x2d.shape
    widths = [s[1] - s[0] for s in (SEG_A, SEG_B, SEG_C, SEG_D, SEG_E)]
    return pl.pallas_call(
        _inproj_body,
        grid=(t // tm,),
        in_specs=[pl.BlockSpec((tm, d), lambda i: (i, 0)),
                  pl.BlockSpec((1, d), lambda i: (0, 0)),
                  pl.BlockSpec((d, IN_PAD), lambda i: (0, 0))],
        out_specs=[pl.BlockSpec((tm, wd), lambda i: (i, 0)) for wd in widths],
        out_shape=[jax.ShapeDtypeStruct((t, wd), F32) for wd in widths],
        compiler_params=pltpu.CompilerParams(dimension_semantics=("arbitrary",), vmem_limit_bytes=VMEM_LIMIT),
        name="inproj",
    )(x2d, nw, w)


def _swa_body(q_ref, kp_ref, kc_ref, vp_ref, vc_ref, num_ref, m_ref, den_ref, *, dil, tq):
    n = pl.program_id(1)
    q = q_ref[0] * (HEAD_DIM ** -0.5)
    k = jnp.concatenate([kp_ref[0], kc_ref[0]], axis=0)
    v = jnp.concatenate([vp_ref[0], vc_ref[0]], axis=0)
    iq = lax.broadcasted_iota(jnp.int32, (tq, 2 * tq), 0)
    jk = lax.broadcasted_iota(jnp.int32, (tq, 2 * tq), 1)
    dist = iq + tq - jk
    valid = (dist >= 0) & (dist <= ATTN_STEPS) & ((jk >= tq) | (n > 0))
    bias = (dist * dil).astype(F32)
    nums, ms, dens = [], [], []
    for h in range(N_HEADS):
        sl = slice(h * HEAD_DIM, (h + 1) * HEAD_DIM)
        s = lax.dot_general(q[:, sl].astype(BF16), k[:, sl].astype(BF16), (((1,), (1,)), ((), ())),
                            preferred_element_type=F32)
        s = s - ALIBI_SLOPES[h] * bias
        s = jnp.where(valid, s, NEG_MASK)
        m = jnp.max(s, axis=-1, keepdims=True)
        p = jnp.exp(s - m)
        den = jnp.sum(p, axis=-1, keepdims=True)
        nums.append(jnp.dot(p.astype(BF16), v[:, sl].astype(BF16), preferred_element_type=F32))
        ms.append(jnp.broadcast_to(m, (tq, HEAD_DIM)))
        dens.append(jnp.broadcast_to(den, (tq, HEAD_DIM)))
    num_ref[0] = jnp.concatenate(nums, axis=-1)
    m_ref[0] = jnp.concatenate(ms, axis=-1)
    den_ref[0] = jnp.concatenate(dens, axis=-1)


def _swa(q, k, v, dil):
    z, n, w = q.shape
    tq = ATTN_STEPS
    cur = pl.BlockSpec((1, tq, w), lambda zi, ni: (zi, ni, 0))
    prev = pl.BlockSpec((1, tq, w), lambda zi, ni: (zi, jnp.maximum(ni - 1, 0), 0))
    shp = jax.ShapeDtypeStruct((z, n, w), F32)
    return pl.pallas_call(
        functools.partial(_swa_body, dil=dil, tq=tq),
        grid=(z, n // tq),
        in_specs=[cur, prev, cur, prev, cur],
        out_specs=[cur, cur, cur],
        out_shape=[shp, shp, shp],
        compiler_params=pltpu.CompilerParams(dimension_semantics=("arbitrary", "arbitrary"),
                                             vmem_limit_bytes=VMEM_LIMIT),
        name=f"swa_d{dil}",
    )(q, k, k, v, v)


def _attn_merge_body(*refs):
    n_g = len(ATTN_PATTERNS)
    nums, ms, dens = refs[0:n_g], refs[n_g:2 * n_g], refs[2 * n_g:3 * n_g]
    gate_ref, out_ref = refs[3 * n_g], refs[3 * n_g + 1]
    m_all = ms[0][...]
    for r in ms[1:]:
        m_all = jnp.maximum(m_all, r[...])
    numer = jnp.zeros_like(m_all)
    denom = jnp.zeros_like(m_all)
    for nr, mr, dr in zip(nums, ms, dens):
        wgt = jnp.exp(mr[...] - m_all)
        numer = numer + wgt * nr[...]
        denom = denom + wgt * dr[...]
    out_ref[...] = (numer / denom) * _silu(gate_ref[...])


def _attn_merge(nums, ms, dens, pa, tm):
    t = pa.shape[0]
    blk = pl.BlockSpec((tm, BRANCH), lambda i: (i, 0))
    gate = pl.BlockSpec((tm, BRANCH), lambda i: (i, 3))
    return pl.pallas_call(
        _attn_merge_body,
        grid=(t // tm,),
        in_specs=[blk] * (3 * len(ATTN_PATTERNS)) + [gate],
        out_specs=blk,
        out_shape=jax.ShapeDtypeStruct((t, BRANCH), F32),
        compiler_params=pltpu.CompilerParams(dimension_semantics=("arbitrary",), vmem_limit_bytes=VMEM_LIMIT),
        name="attn_merge",
    )(*nums, *ms, *dens, pa)


def _attn_sample_body(pa_ref, kc_ref, vc_ref, out_ref, *, t_new, buf):
    rows = N_HEADS * t_new
    pa = pa_ref[0]
    q = pa[:, 0:BRANCH] * (HEAD_DIM ** -0.5)
    k_new = pa[:, BRANCH:2 * BRANCH]
    v_new = pa[:, 2 * BRANCH:3 * BRANCH]
    gate = pa[:, 3 * BRANCH:4 * BRANCH]
    row_h = lax.broadcasted_iota(jnp.int32, (rows, BRANCH), 0) // t_new
    lane_h = lax.broadcasted_iota(jnp.int32, (rows, BRANCH), 1) // HEAD_DIM
    own = row_h == lane_h
    q_bd = jnp.where(own, jnp.concatenate([q] * N_HEADS, axis=0), 0.0).astype(BF16)
    pad = jnp.zeros((LANES - t_new, BRANCH), F32)
    k_new_p = jnp.concatenate([k_new, pad], axis=0).astype(BF16)
    v_new_p = jnp.concatenate([v_new, pad], axis=0).astype(BF16)
    kc = kc_ref[0].astype(BF16)
    vc = vc_ref[0].astype(BF16)
    nt = (((1,), (1,)), ((), ()))
    s = jnp.concatenate([lax.dot_general(q_bd, kc, nt, preferred_element_type=F32),
                         lax.dot_general(q_bd, k_new_p, nt, preferred_element_type=F32)], axis=-1)
    ncol = buf + LANES
    col = lax.broadcasted_iota(jnp.int32, (rows, ncol), 1)
    row = lax.broadcasted_iota(jnp.int32, (rows, ncol), 0)
    delta = buf + row % t_new - col
    hrow = row // t_new
    slope = jnp.where(hrow == 0, ALIBI_SLOPES[0],
                      jnp.where(hrow == 1, ALIBI_SLOPES[1], jnp.where(hrow == 2, ALIBI_SLOPES[2], ALIBI_SLOPES[3])))
    s = s - slope * delta.astype(F32)
    ps, ms, dens = [], [], []
    for win, dil in ATTN_PATTERNS:
        valid = (delta >= 0) & (delta <= win) & ((delta & (dil - 1)) == 0)
        sp = jnp.where(valid, s, NEG_MASK)
        m = jnp.max(sp, axis=-1, keepdims=True)
        p = jnp.exp(sp - m)
        ps.append(p.astype(BF16))
        ms.append(m)
        dens.append(jnp.sum(p, axis=-1, keepdims=True))
    p_all = jnp.concatenate(ps, axis=0)
    num_all = (jnp.dot(p_all[:, :buf], vc, preferred_element_type=F32)
               + jnp.dot(p_all[:, buf:], v_new_p, preferred_element_type=F32))
    m_all = jnp.maximum(jnp.maximum(ms[0], ms[1]), ms[2])
    numer = jnp.zeros((rows, BRANCH), F32)
    denom = jnp.zeros((rows, 1), F32)
    for g in range(len(ATTN_PATTERNS)):
        wgt = jnp.exp(ms[g] - m_all)
        numer = numer + wgt * num_all[g * rows:(g + 1) * rows]
        denom = denom + wgt * dens[g]
    o_full = jnp.where(own, numer / denom, 0.0)
    o = o_full[0:t_new]
    for h in range(1, N_HEADS):
        o = o + o_full[h * t_new:(h + 1) * t_new]
    out_ref[0] = o * _silu(gate)


def _attn_sample(pa, cache_k, cache_v, layer):
    b, t_new, _ = pa.shape
    buf = cache_k.shape[2]
    cache = pl.BlockSpec((None, 1, buf, BRANCH), lambda i: (layer, i, 0, 0))
    return pl.pallas_call(
        functools.partial(_attn_sample_body, t_new=t_new, buf=buf),
        grid=(b,),
        in_specs=[pl.BlockSpec((1, t_new, 4 * BRANCH), lambda i: (i, 0, 0)), cache, cache],
        out_specs=pl.BlockSpec((1, t_new, BRANCH), lambda i: (i, 0, 0)),
        out_shape=jax.ShapeDtypeStruct((b, t_new, BRANCH), F32),
        compiler_params=pltpu.CompilerParams(dimension_semantics=("arbitrary",), vmem_limit_bytes=VMEM_LIMIT),
        name="attn_sample",
    )(pa, cache_k, cache_v)


def _delta_body(pb_ref, pe_ref, cpre_ref, s0_ref, cw_ref, alog_ref, dtb_ref, nw_ref, y_ref, s_out_ref,
                xbuf, s_scr, *, ns, tl, chunk, n_tiles):
    t = pl.program_id(1)
    nc = tl // chunk
    nb = ns * nc
    rows = ns * tl
    width = 3 * BRANCH

    @pl.when(t == 0)
    def _():
        xbuf[:, 0:SUBLANES, :] = cpre_ref[...]
        s_scr[...] = s0_ref[...]

    xbuf[:, SUBLANES:SUBLANES + tl, :] = pb_ref[:, :, 0:width]
    conv = jnp.zeros((ns, tl, width), F32)
    for tap in range(CONV_TAPS):
        off = SUBLANES - (CONV_TAPS - 1) + tap
        conv = conv + xbuf[:, off:off + tl, :] * cw_ref[tap:tap + 1, :]
    if n_tiles > 1:
        xbuf[:, 0:SUBLANES, :] = xbuf[:, tl:tl + SUBLANES, :]
    conv = _silu(conv).reshape(rows, width)
    g_ones = _head_ones()
    q = conv[:, 0:BRANCH]
    k = conv[:, BRANCH:2 * BRANCH]
    v = conv[:, 2 * BRANCH:3 * BRANCH]
    q = q * lax.rsqrt(_head_sum(q * q, g_ones) + 1e-6) * (HEAD_DIM ** -0.5)
    k = k * lax.rsqrt(_head_sum(k * k, g_ones) + 1e-6)
    pe = pe_ref[...].reshape(rows, LANES)
    lane = lax.broadcasted_iota(jnp.int32, (rows, LANES), 1)
    gb = jnp.where(lane < N_HEADS, _sigmoid(pe), -jnp.exp(alog_ref[...]) * _softplus(pe + dtb_ref[...]))

    ri = lax.broadcasted_iota(jnp.int32, (nb, chunk, chunk), 1)
    ci = lax.broadcasted_iota(jnp.int32, (nb, chunk, chunk), 2)
    ltri = jnp.where(ri >= ci, 1.0, 0.0).astype(F32)
    eye = jnp.where(ri == ci, 1.0, 0.0).astype(F32)
    outs = []
    for h in range(N_HEADS):
        sl = slice(h * HEAD_DIM, (h + 1) * HEAD_DIM)
        qh = q[:, sl].reshape(nb, chunk, HEAD_DIM)
        kh = k[:, sl].reshape(nb, chunk, HEAD_DIM)
        vh = v[:, sl].reshape(nb, chunk, HEAD_DIM)
        beta = gb[:, h:h + 1].reshape(nb, chunk, 1)
        g = gb[:, N_HEADS + h:N_HEADS + h + 1].reshape(nb, chunk, 1)
        dmat = _bmm_hi(ltri, jnp.where(ri > ci, jnp.broadcast_to(g, (nb, chunk, chunk)), 0.0))
        gc = dmat[:, :, 0:1] + g[:, 0:1, :]
        g_last = gc[:, chunk - 1:chunk, :]
        decay = jnp.where(ri >= ci, jnp.exp(dmat), 0.0)
        kb = kh * beta
        vb = vh * beta
        kk = jnp.einsum('bid,bjd->bij', kb, kh, preferred_element_type=F32, precision=HIGHEST)
        a_mat = jnp.where(ri > ci, kk * decay, 0.0)
        t_inv = eye - a_mat
        a_pow = a_mat
        for _ in range(int(math.log2(chunk)) - 1):
            a_pow = _bmm_hi(a_pow, a_pow)
            t_inv = _bmm_hi(t_inv, eye + a_pow)
        uw = _bmm_hi(t_inv, jnp.concatenate([vb, kb * jnp.exp(gc)], axis=-1))
        u = uw[:, :, 0:HEAD_DIM]
        w = uw[:, :, HEAD_DIM:2 * HEAD_DIM]
        qk = _bmm_nt(qh.astype(BF16), kh.astype(BF16)) * decay
        qg = qh * jnp.exp(gc)
        kg = kh * jnp.exp(g_last - gc)
        e_last = jnp.exp(g_last)

        def sel(a, c):
            return a.reshape((ns, nc) + a.shape[1:])[:, c]

        s_h = s_scr[:, h]
        o_chunks = []
        for c in range(nc):
            s_b = s_h.astype(BF16)
            v_new = sel(u, c) - _bmm(sel(w, c).astype(BF16), s_b)
            o_c = _bmm(sel(qg, c).astype(BF16), s_b) + _bmm(sel(qk, c).astype(BF16), v_new.astype(BF16))
            s_h = s_h * sel(e_last, c) + _bmm_tn(sel(kg, c).astype(BF16), v_new.astype(BF16))
            o_chunks.append(o_c)
        s_scr[:, h] = s_h
        o_h = jnp.stack(o_chunks, axis=1) if nc > 1 else o_chunks[0]
        outs.append(o_h.reshape(rows, HEAD_DIM))
    o = jnp.concatenate(outs, axis=-1)
    o = o * lax.rsqrt(_head_sum(o * o, g_ones) * (1.0 / HEAD_DIM) + NORM_EPS) * nw_ref[...]
    gate = pb_ref[:, :, width:width + BRANCH].reshape(rows, BRANCH)
    y_ref[...] = (o * _silu(gate)).reshape(ns, tl, BRANCH)

    @pl.when(t == n_tiles - 1)
    def _():
        s_out_ref[...] = s_scr[...]


def _delta(pb, pe, cpre, s0, cw, alog_row, dtb_row, nw_row, *, ns, tl, chunk):
    b, l, _ = pb.shape
    n_tiles = l // tl
    seq3 = lambda w: pl.BlockSpec((ns, tl, w), lambda bi, ti: (bi, ti, 0))
    const2 = lambda a: pl.BlockSpec(a.shape, lambda bi, ti: (0, 0))
    state = pl.BlockSpec((ns, N_HEADS, HEAD_DIM, HEAD_DIM), lambda bi, ti: (bi, 0, 0, 0))
    return pl.pallas_call(
        functools.partial(_delta_body, ns=ns, tl=tl, chunk=chunk, n_tiles=n_tiles),
        grid=(b // ns, n_tiles),
        in_specs=[seq3(4 * BRANCH), seq3(LANES),
                  pl.BlockSpec((ns, SUBLANES, 3 * BRANCH), lambda bi, ti: (bi, 0, 0)), state,
                  const2(cw), const2(alog_row), const2(dtb_row), const2(nw_row)],
        out_specs=[seq3(BRANCH), state],
        out_shape=[jax.ShapeDtypeStruct((b, l, BRANCH), F32),
                   jax.ShapeDtypeStruct((b, N_HEADS, HEAD_DIM, HEAD_DIM), F32)],
        scratch_shapes=[pltpu.VMEM((ns, SUBLANES + tl, 3 * BRANCH), F32),
                        pltpu.VMEM((ns, N_HEADS, HEAD_DIM, HEAD_DIM), F32)],
        compiler_params=pltpu.CompilerParams(dimension_semantics=("arbitrary", "arbitrary"),
                                             vmem_limit_bytes=VMEM_LIMIT),
        name="delta",
    )(pb, pe, cpre, s0, cw, alog_row, dtb_row, nw_row)


def _hgrn_body(pc_ref, lbraw_ref, s0_ref, nw_ref, y_ref, s_out_ref, s_scr, *, ns, tl, chunk, n_tiles, layer):
    t = pl.program_id(1)
    nc = tl // chunk
    nb = ns * nc
    rows = ns * tl

    @pl.when(t == 0)
    def _():
        s_scr[...] = s0_ref[...]

    raw = lbraw_ref[...]
    e = jnp.exp(raw - jnp.max(raw, axis=0, keepdims=True))
    sm = e / jnp.sum(e, axis=0, keepdims=True)
    lb = jnp.zeros((1, BRANCH), F32)
    for d in range(1, layer + 1):
        lb = lb + sm[d:d + 1, :]

    pc = pc_ref[...].reshape(rows, 4 * BRANCH)
    qh = _silu(pc[:, 0:BRANCH])
    fr = pc[:, BRANCH:2 * BRANCH]
    vh = pc[:, 2 * BRANCH:3 * BRANCH]
    gate = pc[:, 3 * BRANCH:4 * BRANCH]
    f = lb + (1.0 - lb) * _sigmoid(fr)
    log_f = jnp.log(f)
    kh = (1.0 - lb) * _sigmoid(-fr)

    q3 = qh.reshape(nb, chunk, BRANCH)
    k3 = kh.reshape(nb, chunk, BRANCH)
    v3 = vh.reshape(nb, chunk, BRANCH)
    ri = lax.broadcasted_iota(jnp.int32, (nb, chunk, chunk), 1)
    ci = lax.broadcasted_iota(jnp.int32, (nb, chunk, chunk), 2)
    ltri = jnp.where(ri >= ci, 1.0, 0.0).astype(F32)
    gcum = _bmm_hi(ltri, log_f.reshape(nb, chunk, BRANCH))
    g_last = gcum[:, chunk - 1:chunk, :]
    qg = q3 * jnp.exp(gcum)
    kg = k3 * jnp.exp(g_last - gcum)

    g_ones = _head_ones()
    row_i = lax.broadcasted_iota(jnp.int32, (nb, chunk, BRANCH), 1)
    intra = jnp.zeros((nb, chunk, BRANCH), F32)
    for j in range(chunk):
        dec = jnp.exp(jnp.where(row_i >= j, gcum - gcum[:, j:j + 1, :], NEG_MASK))
        a_j = _head_sum((q3 * dec * k3[:, j:j + 1, :]).reshape(rows, BRANCH), g_ones)
        intra = intra + a_j.reshape(nb, chunk, BRANCH) * v3[:, j:j + 1, :]

    e_last = jnp.exp(g_last)

    def sel(a, c):
        return a.reshape((ns, nc) + a.shape[1:])[:, c]

    outs = []
    for h in range(N_HEADS):
        sl = slice(h * HEAD_DIM, (h + 1) * HEAD_DIM)
        st = s_scr[:, h]
        o_chunks = []
        for c in range(nc):
            o_chunks.append(_bmm_nt(sel(qg, c)[:, :, sl].astype(BF16), st.astype(BF16)))
            st = st * sel(e_last, c)[:, :, sl] + _bmm_tn(sel(v3, c)[:, :, sl].astype(BF16),
                                                          sel(kg, c)[:, :, sl].astype(BF16))
        s_scr[:, h] = st
        o_h = jnp.stack(o_chunks, axis=1) if nc > 1 else o_chunks[0]
        outs.append(o_h.reshape(rows, HEAD_DIM))
    o = jnp.concatenate(outs, axis=-1) + intra.reshape(rows, BRANCH)
    o = o * lax.rsqrt(_head_sum(o * o, g_ones) * (1.0 / HEAD_DIM) + NORM_EPS) * nw_ref[...]
    y_ref[...] = (o * _silu(gate)).reshape(ns, tl, BRANCH)

    @pl.when(t == n_tiles - 1)
    def _():
        s_out_ref[...] = s_scr[...]


def _hgrn(pc, lb_raw, s0_t, nw_row, *, ns, tl, chunk, layer):
    b, l, _ = pc.shape
    n_tiles = l // tl
    state = pl.BlockSpec((ns, N_HEADS, HEAD_DIM, HEAD_DIM), lambda bi, ti: (bi, 0, 0, 0))
    return pl.pallas_call(
        functools.partial(_hgrn_body, ns=ns, tl=tl, chunk=chunk, n_tiles=n_tiles, layer=layer),
        grid=(b // ns, n_tiles),
        in_specs=[pl.BlockSpec((ns, tl, 4 * BRANCH), lambda bi, ti: (bi, ti, 0)),
                  pl.BlockSpec(lb_raw.shape, lambda bi, ti: (0, 0)), state,
                  pl.BlockSpec(nw_row.shape, lambda bi, ti: (0, 0))],
        out_specs=[pl.BlockSpec((ns, tl, BRANCH), lambda bi, ti: (bi, ti, 0)), state],
        out_shape=[jax.ShapeDtypeStruct((b, l, BRANCH), F32),
                   jax.ShapeDtypeStruct((b, N_HEADS, HEAD_DIM, HEAD_DIM), F32)],
        scratch_shapes=[pltpu.VMEM((ns, N_HEADS, HEAD_DIM, HEAD_DIM), F32)],
        compiler_params=pltpu.CompilerParams(dimension_semantics=("arbitrary", "arbitrary"),
                                             vmem_limit_bytes=VMEM_LIMIT),
        name="hgrn",
    )(pc, lb_raw, s0_t, nw_row)


def _post_body(x_ref, ya_ref, yb_ref, yc_ref, pd_ref, ppre_ref, pw_ref, ps_ref, wo_ref, fw_ref, out_ref, xbuf,
               *, ns, tl, n_tiles, start_pos, final):
    t = pl.program_id(1)
    rows = ns * tl

    @pl.when(t == 0)
    def _():
        xbuf[:, 0:POOL_MAX, :] = ppre_ref[...]

    xbuf[:, POOL_MAX:POOL_MAX + tl, :] = pd_ref[:, :, 0:BRANCH]

    def back(kk):
        return xbuf[:, POOL_MAX - kk:POOL_MAX - kk + tl, :]

    x0 = back(0)
    sums = {}
    acc = x0
    for kk in range(1, POOL_MAX):
        acc = acc + back(kk)
        if kk + 1 in POOL_WINDOWS:
            sums[kk + 1] = acc
    if n_tiles > 1:
        xbuf[:, 0:POOL_MAX, :] = xbuf[:, tl:tl + POOL_MAX, :]
    group = lax.broadcasted_iota(jnp.int32, (ns, tl, BRANCH), 2) // HEAD_DIM
    pos = start_pos + t * tl + lax.broadcasted_iota(jnp.int32, (ns, tl, BRANCH), 1)
    tot = sums[POOL_WINDOWS[-1]]
    win = jnp.full((ns, tl, BRANCH), POOL_WINDOWS[-1], jnp.int32)
    for gi in range(len(POOL_WINDOWS) - 2, -1, -1):
        tot = jnp.where(group == gi, sums[POOL_WINDOWS[gi]], tot)
        win = jnp.where(group == gi, POOL_WINDOWS[gi], win)
    cnt = jnp.minimum(pos + 1, win).astype(F32)
    pooled = (tot / cnt - x0).reshape(rows, BRANCH)
    gate_d = pd_ref[:, :, BRANCH:2 * BRANCH].reshape(rows, BRANCH)
    yd = jnp.dot(pooled.astype(BF16), pw_ref[...], preferred_element_type=F32) * ps_ref[...] * _silu(gate_d)
    ycat = jnp.concatenate([ya_ref[...].reshape(rows, BRANCH), yb_ref[...].reshape(rows, BRANCH),
                            yc_ref[...].reshape(rows, BRANCH), yd], axis=-1).astype(BF16)
    x_new = x_ref[...].reshape(rows, -1) + jnp.dot(ycat, wo_ref[...], preferred_element_type=F32)
    if final:
        ms = jnp.mean(x_new * x_new, axis=-1, keepdims=True)
        x_new = x_new * lax.rsqrt(ms + NORM_EPS) * fw_ref[...]
    out_ref[...] = x_new.reshape(out_ref.shape)


def _post(x, ya, yb, yc, pd, ppre, pw_bd, ps_row, wo, fw_row, *, ns, tl, start_pos, final):
    b, l, d = x.shape
    n_tiles = l // tl
    seq3 = lambda w: pl.BlockSpec((ns, tl, w), lambda bi, ti: (bi, ti, 0))
    const2 = lambda a: pl.BlockSpec(a.shape, lambda bi, ti: (0, 0))
    return pl.pallas_call(
        functools.partial(_post_body, ns=ns, tl=tl, n_tiles=n_tiles, start_pos=start_pos, final=final),
        grid=(b // ns, n_tiles),
        in_specs=[seq3(d), seq3(BRANCH), seq3(BRANCH), seq3(BRANCH), seq3(2 * BRANCH),
                  pl.BlockSpec((ns, POOL_MAX, BRANCH), lambda bi, ti: (bi, 0, 0)),
                  const2(pw_bd), const2(ps_row), const2(wo), const2(fw_row)],
        out_specs=seq3(d),
        out_shape=jax.ShapeDtypeStruct((b, l, d), F32),
        scratch_shapes=[pltpu.VMEM((ns, POOL_MAX + tl, BRANCH), F32)],
        compiler_params=pltpu.CompilerParams(dimension_semantics=("arbitrary", "arbitrary"),
                                             vmem_limit_bytes=VMEM_LIMIT),
        name="post",
    )(x, ya, yb, yc, pd, ppre, pw_bd, ps_row, wo, fw_row)


def _to_classes(a, dil):
    b, s, w = a.shape
    if dil == 1:
        return a
    return a.reshape(b, s // dil, dil, w).transpose(0, 2, 1, 3).reshape(b * dil, s // dil, w)


def _from_classes(a, dil, b):
    if dil == 1:
        return a
    _, n, w = a.shape
    return a.reshape(b, dil, n, w).transpose(0, 2, 1, 3).reshape(b, n * dil, w)


def _prep_weights(w_in, w_out, pool_w):
    depth, d, _ = w_in.shape
    n_small = 2 * N_HEADS
    main = 8 * BRANCH
    w_perm = jnp.concatenate([
        w_in[:, :, 0:main], w_in[:, :, main + n_small:],
        w_in[:, :, main:main + n_small], jnp.zeros((depth, d, LANES - n_small), w_in.dtype)], axis=-1)
    groups = pool_w.shape[1]
    pw_bd = jnp.zeros((depth, BRANCH, BRANCH), pool_w.dtype)
    for g in range(groups):
        sl = slice(g * HEAD_DIM, (g + 1) * HEAD_DIM)
        pw_bd = pw_bd.at[:, sl, sl].set(pool_w[:, g])
    return w_perm.astype(BF16), w_out.astype(BF16), pw_bd.astype(BF16)


def _trunk(x, start_pos, states, prm, cfg):
    b, l, d = x.shape
    depth = prm['norm_w'].shape[0]
    dt = x.dtype
    ns, tl = cfg['ns'], cfg['tl']
    ks, vs, ds, dcs, hs, ps = [], [], [], [], [], []
    for layer in range(depth):
        if states is None:
            conv_prefix = jnp.zeros((b, CONV_TAPS - 1, 3 * BRANCH), dt)
            s_delta = jnp.zeros((b, N_HEADS, HEAD_DIM, HEAD_DIM), F32)
            s_hgrn = jnp.zeros((b, N_HEADS, HEAD_DIM, HEAD_DIM), F32)
            pool_prefix = jnp.zeros((b, POOL_MAX - 1, BRANCH), dt)
        else:
            c_k, c_v, c_delta, c_conv, c_hgrn, c_pool = states
            conv_prefix, s_delta, s_hgrn, pool_prefix = c_conv[layer], c_delta[layer], c_hgrn[layer], c_pool[layer]
        pa, pb, pc, pd, pe = _inproj(x.reshape(b * l, d), prm['norm_w'][layer][None, :], prm['w_in'][layer],
                                     cfg['tm'])
        pa3 = pa.reshape(b, l, 4 * BRANCH)
        a_k = pa3[:, :, BRANCH:2 * BRANCH]
        a_v = pa3[:, :, 2 * BRANCH:3 * BRANCH]
        if states is None:
            a_q = pa3[:, :, 0:BRANCH]
            nums, ms, dens = [], [], []
            for _, dil in ATTN_PATTERNS:
                num, m, den = _swa(_to_classes(a_q, dil), _to_classes(a_k, dil), _to_classes(a_v, dil), dil)
                nums.append(_from_classes(num, dil, b).reshape(b * l, BRANCH))
                ms.append(_from_classes(m, dil, b).reshape(b * l, BRANCH))
                dens.append(_from_classes(den, dil, b).reshape(b * l, BRANCH))
            ya = _attn_merge(nums, ms, dens, pa, cfg['tm']).reshape(b, l, BRANCH)
            keep = min(ATTN_PATTERNS[-1][0], l)
            k_rows, v_rows = a_k[:, l - keep:], a_v[:, l - keep:]
        else:
            ya = _attn_sample(pa3, c_k, c_v, layer)
            k_rows, v_rows = a_k, a_v
        k_rows = k_rows.reshape(k_rows.shape[:2] + (N_HEADS, HEAD_DIM))
        v_rows = v_rows.reshape(v_rows.shape[:2] + (N_HEADS, HEAD_DIM))

        pb3 = pb.reshape(b, l, 4 * BRANCH)
        cpre = jnp.pad(conv_prefix, ((0, 0), (SUBLANES - (CONV_TAPS - 1), 0), (0, 0)))
        yb, sd_new = _delta(pb3, pe.reshape(b, l, LANES), cpre, s_delta, prm['delta_conv_w'][layer],
                            prm['alog_row'][layer], prm['dtb_row'][layer], prm['delta_nw_row'][layer],
                            ns=ns, tl=tl, chunk=math.gcd(l, DELTA_CHUNK))
        conv_new = jnp.concatenate([conv_prefix, pb3[:, :, 0:3 * BRANCH]], axis=1)[:, -(CONV_TAPS - 1):]

        yc, sh_new_t = _hgrn(pc.reshape(b, l, 4 * BRANCH), prm['hgrn_lb_raw'], jnp.swapaxes(s_hgrn, -1, -2),
                             prm['hgrn_nw_row'][layer], ns=ns, tl=tl, chunk=math.gcd(l, HGRN_CHUNK), layer=layer)
        sh_new = jnp.swapaxes(sh_new_t, -1, -2)

        pd3 = pd.reshape(b, l, 2 * BRANCH)
        ppre = jnp.pad(pool_prefix, ((0, 0), (1, 0), (0, 0)))
        pool_new = jnp.concatenate([pool_prefix, pd3[:, :, 0:BRANCH]], axis=1)[:, -(POOL_MAX - 1):]
        x = _post(x, ya, yb, yc, pd3, ppre, prm['pool_w_bd'][layer], prm['pool_scale'][layer][None, :],
                  prm['w_out'][layer], prm['final_norm_w'][None, :], ns=ns, tl=tl, start_pos=start_pos,
                  final=(layer == depth - 1))
        ks.append(k_rows)
        vs.append(v_rows)
        ds.append(sd_new)
        dcs.append(conv_new)
        hs.append(sh_new)
        ps.append(pool_new)
    return x, (jnp.stack(ks), jnp.stack(vs), jnp.stack(ds), jnp.stack(dcs), jnp.stack(hs), jnp.stack(ps))


def _lane_row(vals, offset):
    depth, n = vals.shape
    return jnp.zeros((depth, 1, LANES), F32).at[:, 0, offset:offset + n].set(vals.astype(F32))


def kernel(x_prompt, x_sample, cache_attn_k, cache_attn_v, state_delta, state_delta_conv, state_hgrn, state_pool,
           norm_w, w_in, w_out, delta_conv_w, delta_a_log, delta_dt_bias, delta_norm_w, hgrn_lb_raw, hgrn_norm_w,
           pool_w, pool_scale, final_norm_w):
    w_in_p, w_out_b, pw_bd = _prep_weights(w_in, w_out, pool_w)
    prm = dict(
        norm_w=norm_w, w_in=w_in_p, w_out=w_out_b, pool_w_bd=pw_bd, pool_scale=pool_scale,
        final_norm_w=final_norm_w, delta_conv_w=delta_conv_w, hgrn_lb_raw=hgrn_lb_raw,
        alog_row=_lane_row(delta_a_log, N_HEADS), dtb_row=_lane_row(delta_dt_bias, N_HEADS),
        delta_nw_row=jnp.tile(delta_norm_w, (1, N_HEADS))[:, None, :],
        hgrn_nw_row=jnp.tile(hgrn_norm_w, (1, N_HEADS))[:, None, :],
    )
    y_p, st_p = _trunk(x_prompt, 0, None, prm, dict(tm=256, ns=1, tl=256))
    dec_b, dec_l = x_sample.shape[0], x_sample.shape[1]
    depth, _, buf = cache_attn_k.shape[0:3]
    ck = cache_attn_k.reshape(depth, dec_b, buf, BRANCH)
    cv = cache_attn_v.reshape(depth, dec_b, buf, BRANCH)
    y_s, st_s = _trunk(x_sample, PAST_LEN, (ck, cv, state_delta, state_delta_conv, state_hgrn, state_pool), prm,
                       dict(tm=256, ns=32, tl=dec_l))
    k_p, v_p, d_p, dc_p, h_p, pl_p = st_p
    k_s, v_s, d_s, dc_s, h_s, pl_s = st_s
    return (y_p, y_s, k_p, k_s, v_p, v_s, d_p, d_s, dc_p, dc_s, h_p, h_s, pl_p, pl_s)
```

```python
import functools
import math

import jax
import jax.numpy as jnp
from jax import lax
from jax.experimental import pallas as pl
from jax.experimental.pallas import tpu as pltpu

F32 = jnp.float32
BF16 = jnp.bfloat16

N_HEADS = 4
HEAD_DIM = 64
BRANCH = N_HEADS * HEAD_DIM
ATTN_PATTERNS = ((128, 1), (512, 4), (2048, 16))
ATTN_STEPS = 128
CONV_TAPS = 4
DELTA_CHUNK = 64
HGRN_CHUNK = 16
POOL_WINDOWS = (2, 4, 8, 16)
POOL_MAX = 16
PAST_LEN = 2048
NORM_EPS = 1e-6
NEG_MASK = -1e30
ALIBI_SLOPES = tuple(2.0 ** (-8.0 * (h + 1) / N_HEADS) for h in range(N_HEADS))

SUBLANES = 8
LANES = 128
VMEM_LIMIT = 56 * 1024 * 1024

SEG_A = (0, 4 * BRANCH)
SEG_B = (4 * BRANCH, 8 * BRANCH)
SEG_C = (8 * BRANCH, 12 * BRANCH)
SEG_D = (12 * BRANCH, 14 * BRANCH)
SEG_E = (14 * BRANCH, 14 * BRANCH + LANES)
IN_PAD = SEG_E[1]
DILATED = tuple(d for _, d in ATTN_PATTERNS if d > 1)


def _sigmoid(x):
    return 1.0 / (1.0 + jnp.exp(-x))


def _silu(x):
    return x * _sigmoid(x)


def _softplus(x):
    return jnp.maximum(x, 0.0) + jnp.log(1.0 + jnp.exp(-jnp.abs(x)))


def _head_ones():
    r = lax.broadcasted_iota(jnp.int32, (BRANCH, BRANCH), 0) // HEAD_DIM
    c = lax.broadcasted_iota(jnp.int32, (BRANCH, BRANCH), 1) // HEAD_DIM
    return jnp.where(r == c, 1.0, 0.0).astype(BF16)


def _split2(x):
    hi = x.astype(BF16)
    return hi, (x - hi.astype(F32)).astype(BF16)


def _head_sum(x, g):
    hi, lo = _split2(x)
    return jnp.dot(hi, g, preferred_element_type=F32) + jnp.dot(lo, g, preferred_element_type=F32)


def _bmm(a, b):
    return jnp.einsum('bij,bjk->bik', a, b, preferred_element_type=F32)


def _bmm_nt(a, b):
    return jnp.einsum('bid,bjd->bij', a, b, preferred_element_type=F32)


def _bmm_tn(a, b):
    return jnp.einsum('bci,bcj->bij', a, b, preferred_element_type=F32)


def _bmm3(a, b):
    ah, al = _split2(a)
    bh, bl = _split2(b)
    return _bmm(ah, bh) + _bmm(ah, bl) + _bmm(al, bh)


def _to_head_batch(x2d, nb, chunk):
    parts = [x2d[:, h * HEAD_DIM:(h + 1) * HEAD_DIM].reshape(nb, 1, chunk, HEAD_DIM) for h in range(N_HEADS)]
    return jnp.concatenate(parts, axis=1).reshape(nb * N_HEADS, chunk, HEAD_DIM)


def _from_head_batch(x, nb, chunk):
    x4 = x.reshape(nb, N_HEADS, chunk, HEAD_DIM)
    return jnp.concatenate([x4[:, h].reshape(nb * chunk, HEAD_DIM) for h in range(N_HEADS)], axis=-1)


def _inproj_body(x_ref, nw_ref, w_ref, pa_ref, pb_ref, pc_ref, pd_ref, pe_ref, *rest, tm):
    x = x_ref[...]
    ms = jnp.mean(x * x, axis=-1, keepdims=True)
    h = (x * lax.rsqrt(ms + NORM_EPS) * nw_ref[...]).astype(BF16)
    for ref, (lo, hi) in ((pb_ref, SEG_B), (pc_ref, SEG_C), (pd_ref, SEG_D), (pe_ref, SEG_E)):
        ref[...] = jnp.dot(h, w_ref[:, lo:hi], preferred_element_type=F32)
    p_a = jnp.dot(h, w_ref[:, SEG_A[0]:SEG_A[1]], preferred_element_type=F32)
    pa_ref[...] = p_a
    if not rest:
        return
    class_refs, slab = rest[:-1], rest[-1]
    n_slab = 3 * BRANCH // LANES
    for c in range(n_slab):
        slab[c] = p_a[:, c * LANES:(c + 1) * LANES]
    for ref, dil in zip(class_refs, DILATED):
        for r in range(dil):
            for c in range(n_slab):
                ref[0, r, :, c * LANES:(c + 1) * LANES] = slab[c, pl.ds(r, tm // dil, stride=dil), :].astype(BF16)


def _inproj(x2d, nw, w, tm, seq_len=None):
    t, d = x2d.shape
    widths = [s[1] - s[0] for s in (SEG_A, SEG_B, SEG_C, SEG_D, SEG_E)]
    out_specs = [pl.BlockSpec((tm, wd), lambda i: (i, 0)) for wd in widths]
    out_shape = [jax.ShapeDtypeStruct((t, wd), F32) for wd in widths]
    if seq_len is not None:
        tps = seq_len // tm
        for dil in DILATED:
            out_specs.append(pl.BlockSpec((1, dil, tm // dil, 3 * BRANCH), lambda i: (i // tps, 0, i % tps, 0)))
            out_shape.append(jax.ShapeDtypeStruct((t // seq_len, dil, seq_len // dil, 3 * BRANCH), BF16))
    return pl.pallas_call(
        functools.partial(_inproj_body, tm=tm),
        grid=(t // tm,),
        in_specs=[pl.BlockSpec((tm, d), lambda i: (i, 0)),
                  pl.BlockSpec((1, d), lambda i: (0, 0)),
                  pl.BlockSpec((d, IN_PAD), lambda i: (0, 0))],
        out_specs=out_specs,
        out_shape=out_shape,
        scratch_shapes=[pltpu.VMEM((3 * BRANCH // LANES, tm, LANES), F32)] if seq_len is not None else [],
        compiler_params=pltpu.CompilerParams(dimension_semantics=("arbitrary",), vmem_limit_bytes=VMEM_LIMIT),
        name="inproj",
    )(x2d, nw, w)


def _swa_body(q_ref, kp_ref, kc_ref, vp_ref, vc_ref, o_ref, lse_ref, *, dil, tq):
    n = pl.program_id(1)
    q = (q_ref[...].astype(F32) * (HEAD_DIM ** -0.5)).astype(BF16)
    k = jnp.concatenate([kp_ref[...], kc_ref[...]], axis=0).astype(BF16)
    v = jnp.concatenate([vp_ref[...], vc_ref[...]], axis=0).astype(BF16)
    iq = lax.broadcasted_iota(jnp.int32, (tq, 2 * tq), 0)
    jk = lax.broadcasted_iota(jnp.int32, (tq, 2 * tq), 1)
    dist = iq + tq - jk
    valid = (dist >= 0) & (dist <= ATTN_STEPS) & ((jk >= tq) | (n > 0))
    bias = (dist * dil).astype(F32)
    outs, lses = [], []
    for h in range(N_HEADS):
        sl = slice(h * HEAD_DIM, (h + 1) * HEAD_DIM)
        s = lax.dot_general(q[:, sl], k[:, sl], (((1,), (1,)), ((), ())), preferred_element_type=F32)
        s = s - ALIBI_SLOPES[h] * bias
        s = jnp.where(valid, s, NEG_MASK)
        m = jnp.max(s, axis=-1, keepdims=True)
        p = jnp.exp(s - m)
        den = jnp.sum(p, axis=-1, keepdims=True)
        num = jnp.dot(p.astype(BF16), v[:, sl], preferred_element_type=F32)
        outs.append(num / den)
        lses.append(jnp.broadcast_to(m + jnp.log(den), (tq, HEAD_DIM)))
    o_ref[...] = jnp.concatenate(outs, axis=-1)
    lse_ref[...] = jnp.concatenate(lses, axis=-1)


def _swa(qkv, dil):
    b, _, n, _ = qkv.shape
    tq = ATTN_STEPS

    def spec(col, prev):
        def imap(zi, ni):
            return (zi // dil, zi % dil, jnp.maximum(ni - 1, 0) if prev else ni, col)
        return pl.BlockSpec((None, None, tq, BRANCH), imap)

    shp = jax.ShapeDtypeStruct((b, dil, n, BRANCH), F32)
    return pl.pallas_call(
        functools.partial(_swa_body, dil=dil, tq=tq),
        grid=(b * dil, n // tq),
        in_specs=[spec(0, False), spec(1, True), spec(1, False), spec(2, True), spec(2, False)],
        out_specs=[spec(0, False), spec(0, False)],
        out_shape=[shp, shp],
        compiler_params=pltpu.CompilerParams(dimension_semantics=("arbitrary", "arbitrary"),
                                             vmem_limit_bytes=VMEM_LIMIT),
        name=f"swa_d{dil}",
    )(qkv, qkv, qkv, qkv, qkv)


def _attn_merge_body(*refs, tm):
    n_g = len(ATTN_PATTERNS)
    o_refs, lse_refs = refs[0:n_g], refs[n_g:2 * n_g]
    gate_ref, out_ref = refs[2 * n_g], refs[2 * n_g + 1]
    scratch = refs[2 * n_g + 2:]
    outs, lses = [], []
    si = 0
    for (_, dil), o_ref, lse_ref in zip(ATTN_PATTERNS, o_refs, lse_refs):
        if dil == 1:
            outs.append(o_ref[...])
            lses.append(lse_ref[...])
            continue
        ob, lb = scratch[si], scratch[si + 1]
        si += 2
        n_slab = BRANCH // LANES
        for r in range(dil):
            for c in range(n_slab):
                ob[c, pl.ds(r, tm // dil, stride=dil), :] = o_ref[r, :, c * LANES:(c + 1) * LANES]
                lb[c, pl.ds(r, tm // dil, stride=dil), :] = lse_ref[r, :, c * LANES:(c + 1) * LANES]
        outs.append(jnp.concatenate([ob[c] for c in range(n_slab)], axis=-1))
        lses.append(jnp.concatenate([lb[c] for c in range(n_slab)], axis=-1))
    l_all = lses[0]
    for l in lses[1:]:
        l_all = jnp.maximum(l_all, l)
    numer = jnp.zeros_like(l_all)
    denom = jnp.zeros_like(l_all)
    for o, l in zip(outs, lses):
        wgt = jnp.exp(l - l_all)
        numer = numer + wgt * o
        denom = denom + wgt
    out_ref[...] = (numer / denom) * _silu(gate_ref[...])


def _attn_merge(os_, lses, pa3, tm):
    b, l, _ = pa3.shape

    def spec(dil):
        if dil == 1:
            return pl.BlockSpec((None, None, tm, BRANCH), lambda bi, ti: (bi, 0, ti, 0))
        return pl.BlockSpec((None, dil, tm // dil, BRANCH), lambda bi, ti: (bi, 0, ti, 0))

    specs = [spec(d) for _, d in ATTN_PATTERNS]
    return pl.pallas_call(
        functools.partial(_attn_merge_body, tm=tm),
        grid=(b, l // tm),
        in_specs=specs + specs + [pl.BlockSpec((None, tm, BRANCH), lambda bi, ti: (bi, ti, 3))],
        out_specs=pl.BlockSpec((None, tm, BRANCH), lambda bi, ti: (bi, ti, 0)),
        out_shape=jax.ShapeDtypeStruct((b, l, BRANCH), F32),
        scratch_shapes=[pltpu.VMEM((BRANCH // LANES, tm, LANES), F32)] * (2 * len(DILATED)),
        compiler_params=pltpu.CompilerParams(dimension_semantics=("arbitrary", "arbitrary"),
                                             vmem_limit_bytes=VMEM_LIMIT),
        name="attn_merge",
    )(*os_, *lses, pa3)


def _attn_sample_body(pa_ref, kt_ref, vt_ref, out_ref, *, t_new, buf):
    rows = N_HEADS * t_new
    pa = pa_ref[0]
    q = pa[:, 0:BRANCH] * (HEAD_DIM ** -0.5)
    k_new = pa[:, BRANCH:2 * BRANCH]
    v_new = pa[:, 2 * BRANCH:3 * BRANCH]
    gate = pa[:, 3 * BRANCH:4 * BRANCH]
    row_h = lax.broadcasted_iota(jnp.int32, (rows, BRANCH), 0) // t_new
    lane_h = lax.broadcasted_iota(jnp.int32, (rows, BRANCH), 1) // HEAD_DIM
    own = row_h == lane_h
    q_bd = jnp.where(own, jnp.concatenate([q] * N_HEADS, axis=0), 0.0).astype(BF16)
    pad = jnp.zeros((LANES - t_new, BRANCH), F32)
    k_new_p = jnp.concatenate([k_new, pad], axis=0).astype(BF16)
    v_new_p = jnp.concatenate([v_new, pad], axis=0).astype(BF16)
    kt = kt_ref[...].astype(BF16)
    vt = vt_ref[...].astype(BF16)
    nt = (((1,), (1,)), ((), ()))
    s = jnp.concatenate([jnp.dot(q_bd, kt, preferred_element_type=F32),
                         lax.dot_general(q_bd, k_new_p, nt, preferred_element_type=F32)], axis=-1)
    ncol = buf + LANES
    col = lax.broadcasted_iota(jnp.int32, (rows, ncol), 1)
    row = lax.broadcasted_iota(jnp.int32, (rows, ncol), 0)
    delta = buf + row % t_new - col
    hrow = row // t_new
    slope = jnp.where(hrow == 0, ALIBI_SLOPES[0],
                      jnp.where(hrow == 1, ALIBI_SLOPES[1], jnp.where(hrow == 2, ALIBI_SLOPES[2], ALIBI_SLOPES[3])))
    s = s - slope * delta.astype(F32)
    ps, ms, dens = [], [], []
    for win, dil in ATTN_PATTERNS:
        valid = (delta >= 0) & (delta <= win) & ((delta & (dil - 1)) == 0)
        sp = jnp.where(valid, s, NEG_MASK)
        m = jnp.max(sp, axis=-1, keepdims=True)
        p = jnp.exp(sp - m)
        ps.append(p.astype(BF16))
        ms.append(m)
        dens.append(jnp.sum(p, axis=-1, keepdims=True))
    p_all = jnp.concatenate(ps, axis=0)
    num_all = (lax.dot_general(p_all[:, :buf], vt, nt, preferred_element_type=F32)
               + jnp.dot(p_all[:, buf:], v_new_p, preferred_element_type=F32))
    m_all = jnp.maximum(jnp.maximum(ms[0], ms[1]), ms[2])
    numer = jnp.zeros((rows, BRANCH), F32)
    denom = jnp.zeros((rows, 1), F32)
    for g in range(len(ATTN_PATTERNS)):
        wgt = jnp.exp(ms[g] - m_all)
        numer = numer + wgt * num_all[g * rows:(g + 1) * rows]
        denom = denom + wgt * dens[g]
    o_full = jnp.where(own, numer / denom, 0.0)
    o = o_full[0:t_new]
    for h in range(1, N_HEADS):
        o = o + o_full[h * t_new:(h + 1) * t_new]
    out_ref[0] = o * _silu(gate)


def _attn_sample(pa, cache_kt, cache_vt, layer):
    b, t_new, _ = pa.shape
    buf = cache_kt.shape[3]
    cache = pl.BlockSpec((None, None, BRANCH, buf), lambda i: (layer, i, 0, 0))
    return pl.pallas_call(
        functools.partial(_attn_sample_body, t_new=t_new, buf=buf),
        grid=(b,),
        in_specs=[pl.BlockSpec((1, t_new, 4 * BRANCH), lambda i: (i, 0, 0)), cache, cache],
        out_specs=pl.BlockSpec((1, t_new, BRANCH), lambda i: (i, 0, 0)),
        out_shape=jax.ShapeDtypeStruct((b, t_new, BRANCH), F32),
        compiler_params=pltpu.CompilerParams(dimension_semantics=("arbitrary",), vmem_limit_bytes=VMEM_LIMIT),
        name="attn_sample",
    )(pa, cache_kt, cache_vt)


def _delta_body(pb_ref, pe_ref, cpre_ref, s0_ref, cw_ref, alog_ref, dtb_ref, nw_ref, y_ref, s_out_ref,
                xbuf, s_scr, *, ns, tl, chunk, n_tiles):
    t = pl.program_id(1)
    nc = tl // chunk
    nb = ns * nc
    nbh = nb * N_HEADS
    rows = ns * tl
    width = 3 * BRANCH

    @pl.when(t == 0)
    def _():
        xbuf[:, 0:SUBLANES, :] = cpre_ref[...]
        s_scr[...] = s0_ref[...]

    xbuf[:, SUBLANES:SUBLANES + tl, :] = pb_ref[:, :, 0:width]
    conv = jnp.zeros((ns, tl, width), F32)
    for tap in range(CONV_TAPS):
        off = SUBLANES - (CONV_TAPS - 1) + tap
        conv = conv + xbuf[:, off:off + tl, :] * cw_ref[tap:tap + 1, :]
    if n_tiles > 1:
        xbuf[:, 0:SUBLANES, :] = xbuf[:, tl:tl + SUBLANES, :]
    conv = _silu(conv).reshape(rows, width)
    g_ones = _head_ones()
    q = conv[:, 0:BRANCH]
    k = conv[:, BRANCH:2 * BRANCH]
    v = conv[:, 2 * BRANCH:3 * BRANCH]
    q = q * lax.rsqrt(_head_sum(q * q, g_ones) + 1e-6) * (HEAD_DIM ** -0.5)
    k = k * lax.rsqrt(_head_sum(k * k, g_ones) + 1e-6)
    pe = pe_ref[...].reshape(rows, LANES)
    lane = lax.broadcasted_iota(jnp.int32, (rows, LANES), 1)
    gb = jnp.where(lane < N_HEADS, _sigmoid(pe), -jnp.exp(alog_ref[...]) * _softplus(pe + dtb_ref[...]))

    qh = _to_head_batch(q, nb, chunk)
    kh = _to_head_batch(k, nb, chunk)
    vh = _to_head_batch(v, nb, chunk)

    def col_batch(first):
        cols = [gb[:, first + h:first + h + 1].reshape(nb, 1, chunk, 1) for h in range(N_HEADS)]
        return jnp.concatenate(cols, axis=1).reshape(nbh, chunk, 1)

    beta = col_batch(0)
    g = col_batch(N_HEADS)
    ri = lax.broadcasted_iota(jnp.int32, (nbh, chunk, chunk), 1)
    ci = lax.broadcasted_iota(jnp.int32, (nbh, chunk, chunk), 2)
    ltri = jnp.where(ri >= ci, 1.0, 0.0).astype(BF16)
    eye = jnp.where(ri == ci, 1.0, 0.0).astype(F32)
    gm = jnp.where(ri > ci, jnp.broadcast_to(g, (nbh, chunk, chunk)), 0.0)
    g1 = gm.astype(BF16)
    r1 = gm - g1.astype(F32)
    g2 = r1.astype(BF16)
    g3 = (r1 - g2.astype(F32)).astype(BF16)
    dmat = _bmm(ltri, g1) + _bmm(ltri, g2) + _bmm(ltri, g3)
    gc = dmat[:, :, 0:1] + g[:, 0:1, :]
    g_last = gc[:, chunk - 1:chunk, :]
    decay = jnp.where(ri >= ci, jnp.exp(dmat), 0.0)
    kb = kh * beta
    vb = vh * beta
    kh_b = kh.astype(BF16)
    a_mat = jnp.where(ri > ci, _bmm_nt(kb.astype(BF16), kh_b) * decay, 0.0)
    n_fac = int(math.log2(chunk))
    x = -a_mat
    u = eye + x
    x = _bmm3(x, x)
    for _ in range(2, n_fac):
        prod = _bmm3(jnp.concatenate([x, u], axis=1), x)
        x = prod[:, 0:chunk]
        u = u + prod[:, chunk:2 * chunk]
    t_inv = u + _bmm3(u, x)
    uw = _bmm3(t_inv, jnp.concatenate([vb, kb * jnp.exp(gc)], axis=-1))
    u_mat = uw[:, :, 0:HEAD_DIM]
    w_mat = uw[:, :, HEAD_DIM:2 * HEAD_DIM]
    qk = _bmm_nt(qh.astype(BF16), kh_b) * decay
    qg = qh * jnp.exp(gc)
    kg = kh * jnp.exp(g_last - gc)
    e_last = jnp.exp(g_last)

    def sel(a, c):
        a5 = a.reshape((ns, nc, N_HEADS) + a.shape[1:])
        return a5[:, c].reshape((ns * N_HEADS,) + a.shape[1:])

    st = s_scr[...].reshape(ns * N_HEADS, HEAD_DIM, HEAD_DIM)
    o_chunks = []
    for c in range(nc):
        s_b = st.astype(BF16)
        wq = _bmm(jnp.concatenate([sel(w_mat, c), sel(qg, c)], axis=1).astype(BF16), s_b)
        v_new = (sel(u_mat, c) - wq[:, 0:chunk]).astype(BF16)
        o_chunks.append(wq[:, chunk:2 * chunk] + _bmm(sel(qk, c).astype(BF16), v_new))
        st = st * sel(e_last, c) + _bmm_tn(sel(kg, c).astype(BF16), v_new)
    s_scr[...] = st.reshape(ns, N_HEADS, HEAD_DIM, HEAD_DIM)
    if nc > 1:
        o_all = jnp.concatenate([oc.reshape(ns, 1, N_HEADS, chunk, HEAD_DIM) for oc in o_chunks], axis=1)
        o_all = o_all.reshape(nbh, chunk, HEAD_DIM)
    else:
        o_all = o_chunks[0]
    o = _from_head_batch(o_all, nb, chunk)
    o = o * lax.rsqrt(_head_sum(o * o, g_ones) * (1.0 / HEAD_DIM) + NORM_EPS) * nw_ref[...]
    gate = pb_ref[:, :, width:width + BRANCH].reshape(rows, BRANCH)
    y_ref[...] = (o * _silu(gate)).reshape(ns, tl, BRANCH)

    @pl.when(t == n_tiles - 1)
    def _():
        s_out_ref[...] = s_scr[...]


def _delta(pb, pe, cpre, s0, cw, alog_row, dtb_row, nw_row, *, ns, tl, chunk):
    b, l, _ = pb.shape
    n_tiles = l // tl
    seq3 = lambda w: pl.BlockSpec((ns, tl, w), lambda bi, ti: (bi, ti, 0))
    const2 = lambda a: pl.BlockSpec(a.shape, lambda bi, ti: (0, 0))
    state = pl.BlockSpec((ns, N_HEADS, HEAD_DIM, HEAD_DIM), lambda bi, ti: (bi, 0, 0, 0))
    return pl.pallas_call(
        functools.partial(_delta_body, ns=ns, tl=tl, chunk=chunk, n_tiles=n_tiles),
        grid=(b // ns, n_tiles),
        in_specs=[seq3(4 * BRANCH), seq3(LANES),
                  pl.BlockSpec((ns, SUBLANES, 3 * BRANCH), lambda bi, ti: (bi, 0, 0)), state,
                  const2(cw), const2(alog_row), const2(dtb_row), const2(nw_row)],
        out_specs=[seq3(BRANCH), state],
        out_shape=[jax.ShapeDtypeStruct((b, l, BRANCH), F32),
                   jax.ShapeDtypeStruct((b, N_HEADS, HEAD_DIM, HEAD_DIM), F32)],
        scratch_shapes=[pltpu.VMEM((ns, SUBLANES + tl, 3 * BRANCH), F32),
                        pltpu.VMEM((ns, N_HEADS, HEAD_DIM, HEAD_DIM), F32)],
        compiler_params=pltpu.CompilerParams(dimension_semantics=("arbitrary", "arbitrary"),
                                             vmem_limit_bytes=VMEM_LIMIT),
        name="delta",
    )(pb, pe, cpre, s0, cw, alog_row, dtb_row, nw_row)


def _hgrn_body(pc_ref, lbraw_ref, s0_ref, nw_ref, y_ref, s_out_ref, s_scr, *, ns, tl, chunk, n_tiles, layer):
    t = pl.program_id(1)
    nc = tl // chunk
    nb = ns * nc
    rows = ns * tl

    @pl.when(t == 0)
    def _():
        s_scr[...] = s0_ref[...]

    raw = lbraw_ref[...]
    e = jnp.exp(raw - jnp.max(raw, axis=0, keepdims=True))
    sm = e / jnp.sum(e, axis=0, keepdims=True)
    lb = jnp.zeros((1, BRANCH), F32)
    for d in range(1, layer + 1):
        lb = lb + sm[d:d + 1, :]

    pc = pc_ref[...].reshape(rows, 4 * BRANCH)
    qh = _silu(pc[:, 0:BRANCH])
    fr = pc[:, BRANCH:2 * BRANCH]
    vh = pc[:, 2 * BRANCH:3 * BRANCH]
    gate = pc[:, 3 * BRANCH:4 * BRANCH]
    f = lb + (1.0 - lb) * _sigmoid(fr)
    log_f = jnp.log(f).reshape(nb, chunk, BRANCH)
    kh = (1.0 - lb) * _sigmoid(-fr)

    q3 = qh.reshape(nb, chunk, BRANCH)
    k3 = kh.reshape(nb, chunk, BRANCH)
    v3 = vh.reshape(nb, chunk, BRANCH)
    ri = lax.broadcasted_iota(jnp.int32, (nb, chunk, chunk), 1)
    ci = lax.broadcasted_iota(jnp.int32, (nb, chunk, chunk), 2)
    ltri = jnp.where(ri >= ci, 1.0, 0.0).astype(BF16)
    l1 = log_f.astype(BF16)
    r1 = log_f - l1.astype(F32)
    l2 = r1.astype(BF16)
    l3 = (r1 - l2.astype(F32)).astype(BF16)
    gcum = _bmm(ltri, l1) + _bmm(ltri, l2) + _bmm(ltri, l3)
    g_last = gcum[:, chunk - 1:chunk, :]
    qg = q3 * jnp.exp(gcum)
    kg = k3 * jnp.exp(g_last - gcum)
    e_last = jnp.exp(g_last)

    g_ones = _head_ones()
    row_i = lax.broadcasted_iota(jnp.int32, (nb, chunk, BRANCH), 1)
    intra = jnp.zeros((nb, chunk, BRANCH), F32)
    for j in range(chunk):
        dec = jnp.exp(jnp.where(row_i >= j, gcum - gcum[:, j:j + 1, :], NEG_MASK))
        a_j = _head_sum((q3 * dec * k3[:, j:j + 1, :]).reshape(rows, BRANCH), g_ones)
        intra = intra + a_j.reshape(nb, chunk, BRANCH) * v3[:, j:j + 1, :]

    qg_b = _to_head_batch(qg.reshape(rows, BRANCH), nb, chunk).astype(BF16)
    kg_b = _to_head_batch(kg.reshape(rows, BRANCH), nb, chunk).astype(BF16)
    v_b = _to_head_batch(vh, nb, chunk).astype(BF16)
    el_b = _to_head_batch(e_last.reshape(nb, BRANCH), nb, 1)

    def sel(a, c):
        a5 = a.reshape((ns, nc, N_HEADS) + a.shape[1:])
        return a5[:, c].reshape((ns * N_HEADS,) + a.shape[1:])

    st = s_scr[...].reshape(ns * N_HEADS, HEAD_DIM, HEAD_DIM)
    o_chunks = []
    for c in range(nc):
        o_chunks.append(_bmm_nt(sel(qg_b, c), st.astype(BF16)))
        st = st * sel(el_b, c) + _bmm_tn(sel(v_b, c), sel(kg_b, c))
    s_scr[...] = st.reshape(ns, N_HEADS, HEAD_DIM, HEAD_DIM)
    if nc > 1:
        o_all = jnp.concatenate([oc.reshape(ns, 1, N_HEADS, chunk, HEAD_DIM) for oc in o_chunks], axis=1)
        o_all = o_all.reshape(nb * N_HEADS, chunk, HEAD_DIM)
    else:
        o_all = o_chunks[0]
    o = _from_head_batch(o_all, nb, chunk) + intra.reshape(rows, BRANCH)
    o = o * lax.rsqrt(_head_sum(o * o, g_ones) * (1.0 / HEAD_DIM) + NORM_EPS) * nw_ref[...]
    y_ref[...] = (o * _silu(gate)).reshape(ns, tl, BRANCH)

    @pl.when(t == n_tiles - 1)
    def _():
        s_out_ref[...] = s_scr[...]


def _hgrn(pc, lb_raw, s0_t, nw_row, *, ns, tl, chunk, layer):
    b, l, _ = pc.shape
    n_tiles = l // tl
    state = pl.BlockSpec((ns, N_HEADS, HEAD_DIM, HEAD_DIM), lambda bi, ti: (bi, 0, 0, 0))
    return pl.pallas_call(
        functools.partial(_hgrn_body, ns=ns, tl=tl, chunk=chunk, n_tiles=n_tiles, layer=layer),
        grid=(b // ns, n_tiles),
        in_specs=[pl.BlockSpec((ns, tl, 4 * BRANCH), lambda bi, ti: (bi, ti, 0)),
                  pl.BlockSpec(lb_raw.shape, lambda bi, ti: (0, 0)), state,
                  pl.BlockSpec(nw_row.shape, lambda bi, ti: (0, 0))],
        out_specs=[pl.BlockSpec((ns, tl, BRANCH), lambda bi, ti: (bi, ti, 0)), state],
        out_shape=[jax.ShapeDtypeStruct((b, l, BRANCH), F32),
                   jax.ShapeDtypeStruct((b, N_HEADS, HEAD_DIM, HEAD_DIM), F32)],
        scratch_shapes=[pltpu.VMEM((ns, N_HEADS, HEAD_DIM, HEAD_DIM), F32)],
        compiler_params=pltpu.CompilerParams(dimension_semantics=("arbitrary", "arbitrary"),
                                             vmem_limit_bytes=VMEM_LIMIT),
        name="hgrn",
    )(pc, lb_raw, s0_t, nw_row)


def _post_body(x_ref, ya_ref, yb_ref, yc_ref, pd_ref, ppre_ref, pw_ref, ps_ref, wo_ref, fw_ref, out_ref, xbuf,
               *, ns, tl, n_tiles, start_pos, final):
    t = pl.program_id(1)
    rows = ns * tl

    @pl.when(t == 0)
    def _():
        xbuf[:, 0:POOL_MAX, :] = ppre_ref[...]

    xbuf[:, POOL_MAX:POOL_MAX + tl, :] = pd_ref[:, :, 0:BRANCH]

    def back(kk):
        return xbuf[:, POOL_MAX - kk:POOL_MAX - kk + tl, :]

    x0 = back(0)
    sums = {}
    acc = x0
    for kk in range(1, POOL_MAX):
        acc = acc + back(kk)
        if kk + 1 in POOL_WINDOWS:
            sums[kk + 1] = acc
    if n_tiles > 1:
        xbuf[:, 0:POOL_MAX, :] = xbuf[:, tl:tl + POOL_MAX, :]
    group = lax.broadcasted_iota(jnp.int32, (ns, tl, BRANCH), 2) // HEAD_DIM
    pos = start_pos + t * tl + lax.broadcasted_iota(jnp.int32, (ns, tl, BRANCH), 1)
    tot = sums[POOL_WINDOWS[-1]]
    win = jnp.full((ns, tl, BRANCH), POOL_WINDOWS[-1], jnp.int32)
    for gi in range(len(POOL_WINDOWS) - 2, -1, -1):
        tot = jnp.where(group == gi, sums[POOL_WINDOWS[gi]], tot)
        win = jnp.where(group == gi, POOL_WINDOWS[gi], win)
    cnt = jnp.minimum(pos + 1, win).astype(F32)
    pooled = (tot / cnt - x0).reshape(rows, BRANCH)
    gate_d = pd_ref[:, :, BRANCH:2 * BRANCH].reshape(rows, BRANCH)
    yd = jnp.dot(pooled.astype(BF16), pw_ref[...], preferred_element_type=F32) * ps_ref[...] * _silu(gate_d)
    ycat = jnp.concatenate([ya_ref[...].reshape(rows, BRANCH), yb_ref[...].reshape(rows, BRANCH),
                            yc_ref[...].reshape(rows, BRANCH), yd], axis=-1).astype(BF16)
    x_new = x_ref[...].reshape(rows, -1) + jnp.dot(ycat, wo_ref[...], preferred_element_type=F32)
    if final:
        ms = jnp.mean(x_new * x_new, axis=-1, keepdims=True)
        x_new = x_new * lax.rsqrt(ms + NORM_EPS) * fw_ref[...]
    out_ref[...] = x_new.reshape(out_ref.shape)


def _post(x, ya, yb, yc, pd, ppre, pw_bd, ps_row, wo, fw_row, *, ns, tl, start_pos, final):
    b, l, d = x.shape
    n_tiles = l // tl
    seq3 = lambda w: pl.BlockSpec((ns, tl, w), lambda bi, ti: (bi, ti, 0))
    const2 = lambda a: pl.BlockSpec(a.shape, lambda bi, ti: (0, 0))
    return pl.pallas_call(
        functools.partial(_post_body, ns=ns, tl=tl, n_tiles=n_tiles, start_pos=start_pos, final=final),
        grid=(b // ns, n_tiles),
        in_specs=[seq3(d), seq3(BRANCH), seq3(BRANCH), seq3(BRANCH), seq3(2 * BRANCH),
                  pl.BlockSpec((ns, POOL_MAX, BRANCH), lambda bi, ti: (bi, 0, 0)),
                  const2(pw_bd), const2(ps_row), const2(wo), const2(fw_row)],
        out_specs=seq3(d),
        out_shape=jax.ShapeDtypeStruct((b, l, d), F32),
        scratch_shapes=[pltpu.VMEM((ns, POOL_MAX + tl, BRANCH), F32)],
        compiler_params=pltpu.CompilerParams(dimension_semantics=("arbitrary", "arbitrary"),
                                             vmem_limit_bytes=VMEM_LIMIT),
        name="post",
    )(x, ya, yb, yc, pd, ppre, pw_bd, ps_row, wo, fw_row)


def _prep_weights(w_in, w_out, pool_w):
    depth, d, _ = w_in.shape
    n_small = 2 * N_HEADS
    main = 8 * BRANCH
    w_perm = jnp.concatenate([
        w_in[:, :, 0:main], w_in[:, :, main + n_small:],
        w_in[:, :, main:main + n_small], jnp.zeros((depth, d, LANES - n_small), w_in.dtype)], axis=-1)
    groups = pool_w.shape[1]
    pw_bd = jnp.zeros((depth, BRANCH, BRANCH), pool_w.dtype)
    for g in range(groups):
        sl = slice(g * HEAD_DIM, (g + 1) * HEAD_DIM)
        pw_bd = pw_bd.at[:, sl, sl].set(pool_w[:, g])
    return w_perm.astype(BF16), w_out.astype(BF16), pw_bd.astype(BF16)


def _attn_prompt(pa3, classes, tm):
    b, l, w = pa3.shape
    os_, lses = [], []
    ci = 0
    for _, dil in ATTN_PATTERNS:
        if dil == 1:
            o, lse = _swa(pa3.reshape(b, 1, l, w), 1)
        else:
            o, lse = _swa(classes[ci], dil)
            ci += 1
        os_.append(o)
        lses.append(lse)
    return _attn_merge(os_, lses, pa3, tm)


def _trunk(x, start_pos, states, prm, cfg):
    b, l, d = x.shape
    depth = prm['norm_w'].shape[0]
    dt = x.dtype
    ns, tl = cfg['ns'], cfg['tl']
    ks, vs, ds, dcs, hs, ps = [], [], [], [], [], []
    for layer in range(depth):
        if states is None:
            conv_prefix = jnp.zeros((b, CONV_TAPS - 1, 3 * BRANCH), dt)
            s_delta = jnp.zeros((b, N_HEADS, HEAD_DIM, HEAD_DIM), F32)
            s_hgrn = jnp.zeros((b, N_HEADS, HEAD_DIM, HEAD_DIM), F32)
            pool_prefix = jnp.zeros((b, POOL_MAX - 1, BRANCH), dt)
        else:
            c_kt, c_vt, c_delta, c_conv, c_hgrn, c_pool = states
            conv_prefix, s_delta, s_hgrn, pool_prefix = c_conv[layer], c_delta[layer], c_hgrn[layer], c_pool[layer]
        outs = _inproj(x.reshape(b * l, d), prm['norm_w'][layer][None, :], prm['w_in'][layer], cfg['tm'],
                       seq_len=l if states is None else None)
        pa, pb, pc, pd, pe = outs[0:5]
        pa3 = pa.reshape(b, l, 4 * BRANCH)
        a_k = pa3[:, :, BRANCH:2 * BRANCH]
        a_v = pa3[:, :, 2 * BRANCH:3 * BRANCH]
        if states is None:
            ya = _attn_prompt(pa3, outs[5:], cfg['tm'])
            keep = min(ATTN_PATTERNS[-1][0], l)
            k_rows, v_rows = a_k[:, l - keep:], a_v[:, l - keep:]
        else:
            ya = _attn_sample(pa3, c_kt, c_vt, layer)
            k_rows, v_rows = a_k, a_v
        k_rows = k_rows.reshape(k_rows.shape[:2] + (N_HEADS, HEAD_DIM))
        v_rows = v_rows.reshape(v_rows.shape[:2] + (N_HEADS, HEAD_DIM))

        pb3 = pb.reshape(b, l, 4 * BRANCH)
        cpre = jnp.pad(conv_prefix, ((0, 0), (SUBLANES - (CONV_TAPS - 1), 0), (0, 0)))
        yb, sd_new = _delta(pb3, pe.reshape(b, l, LANES), cpre, s_delta, prm['delta_conv_w'][layer],
                            prm['alog_row'][layer], prm['dtb_row'][layer], prm['delta_nw_row'][layer],
                            ns=ns, tl=tl, chunk=math.gcd(l, DELTA_CHUNK))
        conv_new = jnp.concatenate([conv_prefix, pb3[:, :, 0:3 * BRANCH]], axis=1)[:, -(CONV_TAPS - 1):]

        yc, sh_new_t = _hgrn(pc.reshape(b, l, 4 * BRANCH), prm['hgrn_lb_raw'], jnp.swapaxes(s_hgrn, -1, -2),
                             prm['hgrn_nw_row'][layer], ns=ns, tl=tl, chunk=math.gcd(l, HGRN_CHUNK), layer=layer)
        sh_new = jnp.swapaxes(sh_new_t, -1, -2)

        pd3 = pd.reshape(b, l, 2 * BRANCH)
        ppre = jnp.pad(pool_prefix, ((0, 0), (1, 0), (0, 0)))
        pool_new = jnp.concatenate([pool_prefix, pd3[:, :, 0:BRANCH]], axis=1)[:, -(POOL_MAX - 1):]
        x = _post(x, ya, yb, yc, pd3, ppre, prm['pool_w_bd'][layer], prm['pool_scale'][layer][None, :],
                  prm['w_out'][layer], prm['final_norm_w'][None, :], ns=cfg['ns_post'], tl=cfg['tl_post'],
                  start_pos=start_pos, final=(layer == depth - 1))
        ks.append(k_rows)
        vs.append(v_rows)
        ds.append(sd_new)
        dcs.append(conv_new)
        hs.append(sh_new)
        ps.append(pool_new)
    return x, (jnp.stack(ks), jnp.stack(vs), jnp.stack(ds), jnp.stack(dcs), jnp.stack(hs), jnp.stack(ps))


def _lane_row(vals, offset):
    depth, n = vals.shape
    return jnp.zeros((depth, 1, LANES), F32).at[:, 0, offset:offset + n].set(vals.astype(F32))


def kernel(x_prompt, x_sample, cache_attn_k, cache_attn_v, state_delta, state_delta_conv, state_hgrn, state_pool,
           norm_w, w_in, w_out, delta_conv_w, delta_a_log, delta_dt_bias, delta_norm_w, hgrn_lb_raw, hgrn_norm_w,
           pool_w, pool_scale, final_norm_w):
    w_in_p, w_out_b, pw_bd = _prep_weights(w_in, w_out, pool_w)
    prm = dict(
        norm_w=norm_w, w_in=w_in_p, w_out=w_out_b, pool_w_bd=pw_bd, pool_scale=pool_scale,
        final_norm_w=final_norm_w, delta_conv_w=delta_conv_w, hgrn_lb_raw=hgrn_lb_raw,
        alog_row=_lane_row(delta_a_log, N_HEADS), dtb_row=_lane_row(delta_dt_bias, N_HEADS),
        delta_nw_row=jnp.tile(delta_norm_w, (1, N_HEADS))[:, None, :],
        hgrn_nw_row=jnp.tile(hgrn_norm_w, (1, N_HEADS))[:, None, :],
    )
    bp = x_prompt.shape[0]
    y_p, st_p = _trunk(x_prompt, 0, None, prm, dict(tm=256, ns=bp, tl=256, ns_post=1, tl_post=256))
    dec_b, dec_l = x_sample.shape[0], x_sample.shape[1]
    depth, _, buf = cache_attn_k.shape[0:3]
    ckt = jnp.transpose(cache_attn_k, (0, 1, 3, 4, 2)).reshape(depth, dec_b, BRANCH, buf)
    cvt = jnp.transpose(cache_attn_v, (0, 1, 3, 4, 2)).reshape(depth, dec_b, BRANCH, buf)
    y_s, st_s = _trunk(x_sample, PAST_LEN, (ckt, cvt, state_delta, state_delta_conv, state_hgrn, state_pool), prm,
                       dict(tm=256, ns=32, tl=dec_l, ns_post=32, tl_post=dec_l))
    k_p, v_p, d_p, dc_p, h_p, pl_p = st_p
    k_s, v_s, d_s, dc_s, h_s, pl_s = st_s
    return (y_p, y_s, k_p, k_s, v_p, v_s, d_p, d_s, dc_p, dc_s, h_p, h_s, pl_p, pl_s)
```

```python
import functools
import math

import jax
import jax.numpy as jnp
from jax import lax
from jax.experimental import pallas as pl
from jax.experimental.pallas import tpu as pltpu

F32 = jnp.float32
BF16 = jnp.bfloat16

N_HEADS = 4
HEAD_DIM = 64
BRANCH = N_HEADS * HEAD_DIM
ATTN_PATTERNS = ((128, 1), (512, 4), (2048, 16))
ATTN_STEPS = 128
CONV_TAPS = 4
DELTA_CHUNK = 64
HGRN_CHUNK = 16
POOL_WINDOWS = (2, 4, 8, 16)
POOL_MAX = 16
PAST_LEN = 2048
NORM_EPS = 1e-6
NEG_MASK = -1e30
ALIBI_SLOPES = tuple(2.0 ** (-8.0 * (h + 1) / N_HEADS) for h in range(N_HEADS))

SUBLANES = 8
LANES = 128
VMEM_LIMIT = 56 * 1024 * 1024

SEG_A = (0, 4 * BRANCH)
SEG_B = (4 * BRANCH, 8 * BRANCH)
SEG_C = (8 * BRANCH, 12 * BRANCH)
SEG_D = (12 * BRANCH, 14 * BRANCH)
SEG_E = (14 * BRANCH, 14 * BRANCH + LANES)
IN_PAD = SEG_E[1]
DILATED = tuple(d for _, d in ATTN_PATTERNS if d > 1)


def _sigmoid(x):
    return 1.0 / (1.0 + jnp.exp(-x))


def _silu(x):
    return x * _sigmoid(x)


def _softplus(x):
    return jnp.maximum(x, 0.0) + jnp.log(1.0 + jnp.exp(-jnp.abs(x)))


def _head_ones():
    r = lax.broadcasted_iota(jnp.int32, (BRANCH, BRANCH), 0) // HEAD_DIM
    c = lax.broadcasted_iota(jnp.int32, (BRANCH, BRANCH), 1) // HEAD_DIM
    return jnp.where(r == c, 1.0, 0.0).astype(BF16)


def _split2(x):
    hi = x.astype(BF16)
    return hi, (x - hi.astype(F32)).astype(BF16)


def _head_sum(x, g):
    hi, lo = _split2(x)
    return jnp.dot(hi, g, preferred_element_type=F32) + jnp.dot(lo, g, preferred_element_type=F32)


def _bmm(a, b):
    return jnp.einsum('bij,bjk->bik', a, b, preferred_element_type=F32)


def _bmm_nt(a, b):
    return jnp.einsum('bid,bjd->bij', a, b, preferred_element_type=F32)


def _bmm_tn(a, b):
    return jnp.einsum('bci,bcj->bij', a, b, preferred_element_type=F32)


def _bmm1(a, b):
    return _bmm(a.astype(BF16), b.astype(BF16))


def _bmm3(a, b):
    ah, al = _split2(a)
    bh, bl = _split2(b)
    return _bmm(ah, bh) + _bmm(ah, bl) + _bmm(al, bh)


def _to_head_batch(x2d, nb, chunk):
    parts = [x2d[:, h * HEAD_DIM:(h + 1) * HEAD_DIM].reshape(nb, 1, chunk, HEAD_DIM) for h in range(N_HEADS)]
    return jnp.concatenate(parts, axis=1).reshape(nb * N_HEADS, chunk, HEAD_DIM)


def _from_head_batch(x, nb, chunk):
    x4 = x.reshape(nb, N_HEADS, chunk, HEAD_DIM)
    return jnp.concatenate([x4[:, h].reshape(nb * chunk, HEAD_DIM) for h in range(N_HEADS)], axis=-1)


def _inproj_body(x_ref, nw_ref, w_ref, pa_ref, pb_ref, pc_ref, pd_ref, pe_ref, *rest, tm):
    x = x_ref[...]
    ms = jnp.mean(x * x, axis=-1, keepdims=True)
    h = (x * lax.rsqrt(ms + NORM_EPS) * nw_ref[...]).astype(BF16)
    for ref, (lo, hi) in ((pb_ref, SEG_B), (pc_ref, SEG_C), (pd_ref, SEG_D), (pe_ref, SEG_E)):
        ref[...] = jnp.dot(h, w_ref[:, lo:hi], preferred_element_type=F32)
    p_a = jnp.dot(h, w_ref[:, SEG_A[0]:SEG_A[1]], preferred_element_type=F32)
    pa_ref[...] = p_a
    if not rest:
        return
    class_refs, slab = rest[:-1], rest[-1]
    n_slab = 3 * BRANCH // LANES
    for c in range(n_slab):
        slab[c] = p_a[:, c * LANES:(c + 1) * LANES]
    for ref, dil in zip(class_refs, DILATED):
        for r in range(dil):
            for c in range(n_slab):
                ref[0, r, :, c * LANES:(c + 1) * LANES] = slab[c, pl.ds(r, tm // dil, stride=dil), :].astype(BF16)


def _inproj(x2d, nw, w, tm, seq_len=None):
    t, d = x2d.shape
    widths = [s[1] - s[0] for s in (SEG_A, SEG_B, SEG_C, SEG_D, SEG_E)]
    out_specs = [pl.BlockSpec((tm, wd), lambda i: (i, 0)) for wd in widths]
    out_shape = [jax.ShapeDtypeStruct((t, wd), F32) for wd in widths]
    if seq_len is not None:
        tps = seq_len // tm
        for dil in DILATED:
            out_specs.append(pl.BlockSpec((1, dil, tm // dil, 3 * BRANCH), lambda i: (i // tps, 0, i % tps, 0)))
            out_shape.append(jax.ShapeDtypeStruct((t // seq_len, dil, seq_len // dil, 3 * BRANCH), BF16))
    return pl.pallas_call(
        functools.partial(_inproj_body, tm=tm),
        grid=(t // tm,),
        in_specs=[pl.BlockSpec((tm, d), lambda i: (i, 0)),
                  pl.BlockSpec((1, d), lambda i: (0, 0)),
                  pl.BlockSpec((d, IN_PAD), lambda i: (0, 0))],
        out_specs=out_specs,
        out_shape=out_shape,
        scratch_shapes=[pltpu.VMEM((3 * BRANCH // LANES, tm, LANES), F32)] if seq_len is not None else [],
        compiler_params=pltpu.CompilerParams(dimension_semantics=("arbitrary",), vmem_limit_bytes=VMEM_LIMIT),
        name="inproj",
    )(x2d, nw, w)


def _swa_body(q_ref, kp_ref, kc_ref, vp_ref, vc_ref, o_ref, lse_ref, *, dil, tq, n_sub):
    n = pl.program_id(1)
    q = (q_ref[...].astype(F32) * (HEAD_DIM ** -0.5)).astype(BF16)
    k = jnp.concatenate([kp_ref[...], kc_ref[...]], axis=0).astype(BF16)
    v = jnp.concatenate([vp_ref[...], vc_ref[...]], axis=0).astype(BF16)
    iq = lax.broadcasted_iota(jnp.int32, (tq, 2 * tq), 0)
    jk = lax.broadcasted_iota(jnp.int32, (tq, 2 * tq), 1)
    dist = iq + tq - jk
    in_band = (dist >= 0) & (dist <= ATTN_STEPS)
    bias = (dist * dil).astype(F32)
    for i in range(n_sub):
        valid = (in_band & ((jk >= tq) | (n > 0))) if i == 0 else in_band
        rows = slice(i * tq, (i + 1) * tq)
        keys = slice(i * tq, (i + 2) * tq)
        outs, lses = [], []
        for h in range(N_HEADS):
            sl = slice(h * HEAD_DIM, (h + 1) * HEAD_DIM)
            s = lax.dot_general(q[rows, sl], k[keys, sl], (((1,), (1,)), ((), ())), preferred_element_type=F32)
            s = s - ALIBI_SLOPES[h] * bias
            s = jnp.where(valid, s, NEG_MASK)
            m = jnp.max(s, axis=-1, keepdims=True)
            p = jnp.exp(s - m)
            den = jnp.sum(p, axis=-1, keepdims=True)
            num = jnp.dot(p.astype(BF16), v[keys, sl], preferred_element_type=F32)
            outs.append(num / den)
            lses.append(jnp.broadcast_to(m + jnp.log(den), (tq, HEAD_DIM)))
        o_ref[rows, :] = jnp.concatenate(outs, axis=-1)
        lse_ref[rows, :] = jnp.concatenate(lses, axis=-1)


def _swa(qkv, dil, n_sub):
    b, _, n, _ = qkv.shape
    tq = ATTN_STEPS
    tqb = tq * n_sub

    def cur(col):
        return pl.BlockSpec((None, None, tqb, BRANCH), lambda zi, ni: (zi // dil, zi % dil, ni, col))

    def prev(col):
        return pl.BlockSpec((None, None, tq, BRANCH),
                            lambda zi, ni: (zi // dil, zi % dil, jnp.maximum(ni * n_sub - 1, 0), col))

    shp = jax.ShapeDtypeStruct((b, dil, n, BRANCH), F32)
    return pl.pallas_call(
        functools.partial(_swa_body, dil=dil, tq=tq, n_sub=n_sub),
        grid=(b * dil, n // tqb),
        in_specs=[cur(0), prev(1), cur(1), prev(2), cur(2)],
        out_specs=[cur(0), cur(0)],
        out_shape=[shp, shp],
        compiler_params=pltpu.CompilerParams(dimension_semantics=("arbitrary", "arbitrary"),
                                             vmem_limit_bytes=VMEM_LIMIT),
        name=f"swa_d{dil}",
    )(qkv, qkv, qkv, qkv, qkv)


def _merge_patterns(o_refs, lse_refs, scratch, tm):
    outs, lses = [], []
    si = 0
    for (_, dil), o_ref, lse_ref in zip(ATTN_PATTERNS, o_refs, lse_refs):
        if dil == 1:
            outs.append(o_ref[...])
            lses.append(lse_ref[...])
            continue
        ob, lb = scratch[si], scratch[si + 1]
        si += 2
        n_slab = BRANCH // LANES
        for r in range(dil):
            for c in range(n_slab):
                ob[c, pl.ds(r, tm // dil, stride=dil), :] = o_ref[r, :, c * LANES:(c + 1) * LANES]
                lb[c, pl.ds(r, tm // dil, stride=dil), :] = lse_ref[r, :, c * LANES:(c + 1) * LANES]
        outs.append(jnp.concatenate([ob[c] for c in range(n_slab)], axis=-1))
        lses.append(jnp.concatenate([lb[c] for c in range(n_slab)], axis=-1))
    l_all = lses[0]
    for l in lses[1:]:
        l_all = jnp.maximum(l_all, l)
    numer = jnp.zeros_like(l_all)
    denom = jnp.zeros_like(l_all)
    for o, l in zip(outs, lses):
        wgt = jnp.exp(l - l_all)
        numer = numer + wgt * o
        denom = denom + wgt
    return numer / denom


def _attn_sample_one(pa, kt, vt, t_new, buf):
    rows = N_HEADS * t_new
    q = pa[:, 0:BRANCH] * (HEAD_DIM ** -0.5)
    k_new = pa[:, BRANCH:2 * BRANCH]
    v_new = pa[:, 2 * BRANCH:3 * BRANCH]
    gate = pa[:, 3 * BRANCH:4 * BRANCH]
    row_h = lax.broadcasted_iota(jnp.int32, (rows, BRANCH), 0) // t_new
    lane_h = lax.broadcasted_iota(jnp.int32, (rows, BRANCH), 1) // HEAD_DIM
    own = row_h == lane_h
    q_bd = jnp.where(own, jnp.concatenate([q] * N_HEADS, axis=0), 0.0).astype(BF16)
    pad = jnp.zeros((LANES - t_new, BRANCH), F32)
    k_new_p = jnp.concatenate([k_new, pad], axis=0).astype(BF16)
    v_new_p = jnp.concatenate([v_new, pad], axis=0).astype(BF16)
    kt = kt.astype(BF16)
    vt = vt.astype(BF16)
    nt = (((1,), (1,)), ((), ()))
    s = jnp.concatenate([jnp.dot(q_bd, kt, preferred_element_type=F32),
                         lax.dot_general(q_bd, k_new_p, nt, preferred_element_type=F32)], axis=-1)
    ncol = buf + LANES
    col = lax.broadcasted_iota(jnp.int32, (rows, ncol), 1)
    row = lax.broadcasted_iota(jnp.int32, (rows, ncol), 0)
    delta = buf + row % t_new - col
    hrow = row // t_new
    slope = jnp.where(hrow == 0, ALIBI_SLOPES[0],
                      jnp.where(hrow == 1, ALIBI_SLOPES[1], jnp.where(hrow == 2, ALIBI_SLOPES[2], ALIBI_SLOPES[3])))
    s = s - slope * delta.astype(F32)
    ps, ms, dens = [], [], []
    for win, dil in ATTN_PATTERNS:
        valid = (delta >= 0) & (delta <= win) & ((delta & (dil - 1)) == 0)
        sp = jnp.where(valid, s, NEG_MASK)
        m = jnp.max(sp, axis=-1, keepdims=True)
        p = jnp.exp(sp - m)
        ps.append(p.astype(BF16))
        ms.append(m)
        dens.append(jnp.sum(p, axis=-1, keepdims=True))
    p_all = jnp.concatenate(ps, axis=0)
    num_all = (lax.dot_general(p_all[:, :buf], vt, nt, preferred_element_type=F32)
               + jnp.dot(p_all[:, buf:], v_new_p, preferred_element_type=F32))
    m_all = jnp.maximum(jnp.maximum(ms[0], ms[1]), ms[2])
    numer = jnp.zeros((rows, BRANCH), F32)
    denom = jnp.zeros((rows, 1), F32)
    for g in range(len(ATTN_PATTERNS)):
        wgt = jnp.exp(ms[g] - m_all)
        numer = numer + wgt * num_all[g * rows:(g + 1) * rows]
        denom = denom + wgt * dens[g]
    o_full = jnp.where(own, numer / denom, 0.0)
    o = o_full[0:t_new]
    for h in range(1, N_HEADS):
        o = o + o_full[h * t_new:(h + 1) * t_new]
    return o * _silu(gate)


def _attn_sample_body(pa_ref, kt_ref, vt_ref, out_ref, *, t_new, buf, nb):
    for j in range(nb):
        out_ref[j] = _attn_sample_one(pa_ref[j], kt_ref[j], vt_ref[j], t_new, buf)


def _attn_sample(pa, cache_kt, cache_vt, layer, nb):
    b, t_new, _ = pa.shape
    buf = cache_kt.shape[3]
    cache = pl.BlockSpec((None, nb, BRANCH, buf), lambda i: (layer, i, 0, 0))
    return pl.pallas_call(
        functools.partial(_attn_sample_body, t_new=t_new, buf=buf, nb=nb),
        grid=(b // nb,),
        in_specs=[pl.BlockSpec((nb, t_new, 4 * BRANCH), lambda i: (i, 0, 0)), cache, cache],
        out_specs=pl.BlockSpec((nb, t_new, BRANCH), lambda i: (i, 0, 0)),
        out_shape=jax.ShapeDtypeStruct((b, t_new, BRANCH), F32),
        compiler_params=pltpu.CompilerParams(dimension_semantics=("arbitrary",), vmem_limit_bytes=VMEM_LIMIT),
        name="attn_sample",
    )(pa, cache_kt, cache_vt)


def _delta_body(pb_ref, pe_ref, cpre_ref, s0_ref, cw_ref, alog_ref, dtb_ref, nw_ref, y_ref, s_out_ref,
                xbuf, s_scr, *, ns, tl, chunk, n_tiles):
    t = pl.program_id(1)
    nc = tl // chunk
    nb = ns * nc
    nbh = nb * N_HEADS
    rows = ns * tl
    width = 3 * BRANCH

    @pl.when(t == 0)
    def _():
        xbuf[:, 0:SUBLANES, :] = cpre_ref[...]
        s_scr[...] = s0_ref[...]

    xbuf[:, SUBLANES:SUBLANES + tl, :] = pb_ref[:, :, 0:width]
    conv = jnp.zeros((ns, tl, width), F32)
    for tap in range(CONV_TAPS):
        off = SUBLANES - (CONV_TAPS - 1) + tap
        conv = conv + xbuf[:, off:off + tl, :] * cw_ref[tap:tap + 1, :]
    if n_tiles > 1:
        xbuf[:, 0:SUBLANES, :] = xbuf[:, tl:tl + SUBLANES, :]
    conv = _silu(conv).reshape(rows, width)
    g_ones = _head_ones()
    q = conv[:, 0:BRANCH]
    k = conv[:, BRANCH:2 * BRANCH]
    v = conv[:, 2 * BRANCH:3 * BRANCH]
    q = q * lax.rsqrt(_head_sum(q * q, g_ones) + 1e-6) * (HEAD_DIM ** -0.5)
    k = k * lax.rsqrt(_head_sum(k * k, g_ones) + 1e-6)
    pe = pe_ref[...].reshape(rows, LANES)
    lane = lax.broadcasted_iota(jnp.int32, (rows, LANES), 1)
    gb = jnp.where(lane < N_HEADS, _sigmoid(pe), -jnp.exp(alog_ref[...]) * _softplus(pe + dtb_ref[...]))

    qh = _to_head_batch(q, nb, chunk)
    kh = _to_head_batch(k, nb, chunk)
    vh = _to_head_batch(v, nb, chunk)

    def col_batch(first):
        cols = [gb[:, first + h:first + h + 1].reshape(nb, 1, chunk, 1) for h in range(N_HEADS)]
        return jnp.concatenate(cols, axis=1).reshape(nbh, chunk, 1)

    beta = col_batch(0)
    g = col_batch(N_HEADS)
    ri = lax.broadcasted_iota(jnp.int32, (nbh, chunk, chunk), 1)
    ci = lax.broadcasted_iota(jnp.int32, (nbh, chunk, chunk), 2)
    ltri = jnp.where(ri >= ci, 1.0, 0.0).astype(BF16)
    eye = jnp.where(ri == ci, 1.0, 0.0).astype(F32)
    gm = jnp.where(ri > ci, jnp.broadcast_to(g, (nbh, chunk, chunk)), 0.0)
    g1 = gm.astype(BF16)
    r1 = gm - g1.astype(F32)
    g2 = r1.astype(BF16)
    g3 = (r1 - g2.astype(F32)).astype(BF16)
    dmat = _bmm(ltri, g1) + _bmm(ltri, g2) + _bmm(ltri, g3)
    gc = dmat[:, :, 0:1] + g[:, 0:1, :]
    g_last = gc[:, chunk - 1:chunk, :]
    decay = jnp.where(ri >= ci, jnp.exp(dmat), 0.0)
    kb = kh * beta
    vb = vh * beta
    kh_b = kh.astype(BF16)
    a_mat = jnp.where(ri > ci, _bmm_nt(kb.astype(BF16), kh_b) * decay, 0.0)
    n_fac = int(math.log2(chunk))
    x = -a_mat
    u = eye + x
    x = _bmm1(x, x)
    for _ in range(2, n_fac):
        prod = _bmm1(jnp.concatenate([x, u], axis=1), x)
        x = prod[:, 0:chunk]
        u = u + prod[:, chunk:2 * chunk]
    t_inv = u + _bmm1(u, x)
    uw = _bmm3(t_inv, jnp.concatenate([vb, kb * jnp.exp(gc)], axis=-1))
    u_mat = uw[:, :, 0:HEAD_DIM]
    w_mat = uw[:, :, HEAD_DIM:2 * HEAD_DIM]
    qk = _bmm_nt(qh.astype(BF16), kh_b) * decay
    qg = qh * jnp.exp(gc)
    kg = kh * jnp.exp(g_last - gc)
    e_last = jnp.exp(g_last)

    def sel(a, c):
        a5 = a.reshape((ns, nc, N_HEADS) + a.shape[1:])
        return a5[:, c].reshape((ns * N_HEADS,) + a.shape[1:])

    st = s_scr[...].reshape(ns * N_HEADS, HEAD_DIM, HEAD_DIM)
    o_chunks = []
    for c in range(nc):
        s_b = st.astype(BF16)
        wq = _bmm(jnp.concatenate([sel(w_mat, c), sel(qg, c)], axis=1).astype(BF16), s_b)
        v_new = (sel(u_mat, c) - wq[:, 0:chunk]).astype(BF16)
        o_chunks.append(wq[:, chunk:2 * chunk] + _bmm(sel(qk, c).astype(BF16), v_new))
        st = st * sel(e_last, c) + _bmm_tn(sel(kg, c).astype(BF16), v_new)
    s_scr[...] = st.reshape(ns, N_HEADS, HEAD_DIM, HEAD_DIM)
    if nc > 1:
        o_all = jnp.concatenate([oc.reshape(ns, 1, N_HEADS, chunk, HEAD_DIM) for oc in o_chunks], axis=1)
        o_all = o_all.reshape(nbh, chunk, HEAD_DIM)
    else:
        o_all = o_chunks[0]
    o = _from_head_batch(o_all, nb, chunk)
    o = o * lax.rsqrt(_head_sum(o * o, g_ones) * (1.0 / HEAD_DIM) + NORM_EPS) * nw_ref[...]
    gate = pb_ref[:, :, width:width + BRANCH].reshape(rows, BRANCH)
    y_ref[...] = (o * _silu(gate)).reshape(ns, tl, BRANCH)

    @pl.when(t == n_tiles - 1)
    def _():
        s_out_ref[...] = s_scr[...]


def _delta(pb, pe, cpre, s0, cw, alog_row, dtb_row, nw_row, *, ns, tl, chunk):
    b, l, _ = pb.shape
    n_tiles = l // tl
    seq3 = lambda w: pl.BlockSpec((ns, tl, w), lambda bi, ti: (bi, ti, 0))
    const2 = lambda a: pl.BlockSpec(a.shape, lambda bi, ti: (0, 0))
    state = pl.BlockSpec((ns, N_HEADS, HEAD_DIM, HEAD_DIM), lambda bi, ti: (bi, 0, 0, 0))
    return pl.pallas_call(
        functools.partial(_delta_body, ns=ns, tl=tl, chunk=chunk, n_tiles=n_tiles),
        grid=(b // ns, n_tiles),
        in_specs=[seq3(4 * BRANCH), seq3(LANES),
                  pl.BlockSpec((ns, SUBLANES, 3 * BRANCH), lambda bi, ti: (bi, 0, 0)), state,
                  const2(cw), const2(alog_row), const2(dtb_row), const2(nw_row)],
        out_specs=[seq3(BRANCH), state],
        out_shape=[jax.ShapeDtypeStruct((b, l, BRANCH), F32),
                   jax.ShapeDtypeStruct((b, N_HEADS, HEAD_DIM, HEAD_DIM), F32)],
        scratch_shapes=[pltpu.VMEM((ns, SUBLANES + tl, 3 * BRANCH), F32),
                        pltpu.VMEM((ns, N_HEADS, HEAD_DIM, HEAD_DIM), F32)],
        compiler_params=pltpu.CompilerParams(dimension_semantics=("arbitrary", "arbitrary"),
                                             vmem_limit_bytes=VMEM_LIMIT),
        name="delta",
    )(pb, pe, cpre, s0, cw, alog_row, dtb_row, nw_row)


def _hgrn_body(pc_ref, lbraw_ref, s0_ref, nw_ref, y_ref, s_out_ref, s_scr, *, ns, tl, chunk, n_tiles, layer):
    t = pl.program_id(1)
    nc = tl // chunk
    nb = ns * nc
    rows = ns * tl

    @pl.when(t == 0)
    def _():
        s_scr[...] = s0_ref[...]

    raw = lbraw_ref[...]
    e = jnp.exp(raw - jnp.max(raw, axis=0, keepdims=True))
    sm = e / jnp.sum(e, axis=0, keepdims=True)
    lb = jnp.zeros((1, BRANCH), F32)
    for d in range(1, layer + 1):
        lb = lb + sm[d:d + 1, :]

    pc = pc_ref[...].reshape(rows, 4 * BRANCH)
    qh = _silu(pc[:, 0:BRANCH])
    fr = pc[:, BRANCH:2 * BRANCH]
    vh = pc[:, 2 * BRANCH:3 * BRANCH]
    gate = pc[:, 3 * BRANCH:4 * BRANCH]
    f = lb + (1.0 - lb) * _sigmoid(fr)
    log_f = jnp.log(f).reshape(nb, chunk, BRANCH)
    kh = (1.0 - lb) * _sigmoid(-fr)

    q3 = qh.reshape(nb, chunk, BRANCH)
    k3 = kh.reshape(nb, chunk, BRANCH)
    v3 = vh.reshape(nb, chunk, BRANCH)
    ri = lax.broadcasted_iota(jnp.int32, (nb, chunk, chunk), 1)
    ci = lax.broadcasted_iota(jnp.int32, (nb, chunk, chunk), 2)
    ltri = jnp.where(ri >= ci, 1.0, 0.0).astype(BF16)
    l1 = log_f.astype(BF16)
    r1 = log_f - l1.astype(F32)
    l2 = r1.astype(BF16)
    l3 = (r1 - l2.astype(F32)).astype(BF16)
    gcum = _bmm(ltri, l1) + _bmm(ltri, l2) + _bmm(ltri, l3)
    g_last = gcum[:, chunk - 1:chunk, :]
    qg = q3 * jnp.exp(gcum)
    kg = k3 * jnp.exp(g_last - gcum)
    e_last = jnp.exp(g_last)

    g_ones = _head_ones()
    n_grp = chunk // SUBLANES
    row_i = [g * SUBLANES + lax.broadcasted_iota(jnp.int32, (nb, chunk - g * SUBLANES, BRANCH), 1)
             for g in range(n_grp)]
    acc = [jnp.zeros((nb, SUBLANES, BRANCH), F32) for _ in range(n_grp)]
    for j in range(chunk):
        g0 = j // SUBLANES
        lo = g0 * SUBLANES
        dec = jnp.exp(jnp.where(row_i[g0] >= j, gcum[:, lo:, :] - gcum[:, j:j + 1, :], NEG_MASK))
        t_j = (q3[:, lo:, :] * dec * k3[:, j:j + 1, :]).reshape(nb * (chunk - lo), BRANCH)
        a_j = jnp.dot(t_j.astype(BF16), g_ones, preferred_element_type=F32).reshape(nb, chunk - lo, BRANCH)
        c_j = a_j * v3[:, j:j + 1, :]
        for gi in range(g0, n_grp):
            acc[gi] = acc[gi] + c_j[:, (gi - g0) * SUBLANES:(gi - g0 + 1) * SUBLANES, :]
    intra = jnp.concatenate(acc, axis=1) if n_grp > 1 else acc[0]

    qg_b = _to_head_batch(qg.reshape(rows, BRANCH), nb, chunk).astype(BF16)
    kg_b = _to_head_batch(kg.reshape(rows, BRANCH), nb, chunk).astype(BF16)
    v_b = _to_head_batch(vh, nb, chunk).astype(BF16)
    el_b = _to_head_batch(e_last.reshape(nb, BRANCH), nb, 1)

    def sel(a, c):
        a5 = a.reshape((ns, nc, N_HEADS) + a.shape[1:])
        return a5[:, c].reshape((ns * N_HEADS,) + a.shape[1:])

    st = s_scr[...].reshape(ns * N_HEADS, HEAD_DIM, HEAD_DIM)
    o_chunks = []
    for c in range(nc):
        o_chunks.append(_bmm_nt(sel(qg_b, c), st.astype(BF16)))
        st = st * sel(el_b, c) + _bmm_tn(sel(v_b, c), sel(kg_b, c))
    s_scr[...] = st.reshape(ns, N_HEADS, HEAD_DIM, HEAD_DIM)
    if nc > 1:
        o_all = jnp.concatenate([oc.reshape(ns, 1, N_HEADS, chunk, HEAD_DIM) for oc in o_chunks], axis=1)
        o_all = o_all.reshape(nb * N_HEADS, chunk, HEAD_DIM)
    else:
        o_all = o_chunks[0]
    o = _from_head_batch(o_all, nb, chunk) + intra.reshape(rows, BRANCH)
    o = o * lax.rsqrt(_head_sum(o * o, g_ones) * (1.0 / HEAD_DIM) + NORM_EPS) * nw_ref[...]
    y_ref[...] = (o * _silu(gate)).reshape(ns, tl, BRANCH)

    @pl.when(t == n_tiles - 1)
    def _():
        s_out_ref[...] = s_scr[...]


def _hgrn(pc, lb_raw, s0_t, nw_row, *, ns, tl, chunk, layer):
    b, l, _ = pc.shape
    n_tiles = l // tl
    state = pl.BlockSpec((ns, N_HEADS, HEAD_DIM, HEAD_DIM), lambda bi, ti: (bi, 0, 0, 0))
    return pl.pallas_call(
        functools.partial(_hgrn_body, ns=ns, tl=tl, chunk=chunk, n_tiles=n_tiles, layer=layer),
        grid=(b // ns, n_tiles),
        in_specs=[pl.BlockSpec((ns, tl, 4 * BRANCH), lambda bi, ti: (bi, ti, 0)),
                  pl.BlockSpec(lb_raw.shape, lambda bi, ti: (0, 0)), state,
                  pl.BlockSpec(nw_row.shape, lambda bi, ti: (0, 0))],
        out_specs=[pl.BlockSpec((ns, tl, BRANCH), lambda bi, ti: (bi, ti, 0)), state],
        out_shape=[jax.ShapeDtypeStruct((b, l, BRANCH), F32),
                   jax.ShapeDtypeStruct((b, N_HEADS, HEAD_DIM, HEAD_DIM), F32)],
        scratch_shapes=[pltpu.VMEM((ns, N_HEADS, HEAD_DIM, HEAD_DIM), F32)],
        compiler_params=pltpu.CompilerParams(dimension_semantics=("arbitrary", "arbitrary"),
                                             vmem_limit_bytes=VMEM_LIMIT),
        name="hgrn",
    )(pc, lb_raw, s0_t, nw_row)


def _post_body(*refs, ns, tl, n_tiles, start_pos, final, n_attn):
    attn_refs = refs[0:n_attn]
    (x_ref, yb_ref, yc_ref, pd_ref, ppre_ref, pw_ref, ps_ref, wo_ref, fw_ref, out_ref, xbuf) = refs[n_attn:n_attn + 11]
    slabs = refs[n_attn + 11:]
    t = pl.program_id(1)
    rows = ns * tl
    if n_attn == 1:
        ya = attn_refs[0][...].reshape(rows, BRANCH)
    else:
        n_g = len(ATTN_PATTERNS)
        ya = _merge_patterns(attn_refs[0:n_g], attn_refs[n_g:2 * n_g], slabs, tl) * _silu(attn_refs[2 * n_g][...])

    @pl.when(t == 0)
    def _():
        xbuf[:, 0:POOL_MAX, :] = ppre_ref[...]

    xbuf[:, POOL_MAX:POOL_MAX + tl, :] = pd_ref[:, :, 0:BRANCH]

    def back(kk):
        return xbuf[:, POOL_MAX - kk:POOL_MAX - kk + tl, :]

    x0 = back(0)
    sums = {}
    acc = x0
    for kk in range(1, POOL_MAX):
        acc = acc + back(kk)
        if kk + 1 in POOL_WINDOWS:
            sums[kk + 1] = acc
    if n_tiles > 1:
        xbuf[:, 0:POOL_MAX, :] = xbuf[:, tl:tl + POOL_MAX, :]
    group = lax.broadcasted_iota(jnp.int32, (ns, tl, BRANCH), 2) // HEAD_DIM
    pos = start_pos + t * tl + lax.broadcasted_iota(jnp.int32, (ns, tl, BRANCH), 1)
    tot = sums[POOL_WINDOWS[-1]]
    win = jnp.full((ns, tl, BRANCH), POOL_WINDOWS[-1], jnp.int32)
    for gi in range(len(POOL_WINDOWS) - 2, -1, -1):
        tot = jnp.where(group == gi, sums[POOL_WINDOWS[gi]], tot)
        win = jnp.where(group == gi, POOL_WINDOWS[gi], win)
    cnt = jnp.minimum(pos + 1, win).astype(F32)
    pooled = (tot / cnt - x0).reshape(rows, BRANCH)
    gate_d = pd_ref[:, :, BRANCH:2 * BRANCH].reshape(rows, BRANCH)
    yd = jnp.dot(pooled.astype(BF16), pw_ref[...], preferred_element_type=F32) * ps_ref[...] * _silu(gate_d)
    ycat = jnp.concatenate([ya, yb_ref[...].reshape(rows, BRANCH),
                            yc_ref[...].reshape(rows, BRANCH), yd], axis=-1).astype(BF16)
    x_new = x_ref[...].reshape(rows, -1) + jnp.dot(ycat, wo_ref[...], preferred_element_type=F32)
    if final:
        ms = jnp.mean(x_new * x_new, axis=-1, keepdims=True)
        x_new = x_new * lax.rsqrt(ms + NORM_EPS) * fw_ref[...]
    out_ref[...] = x_new.reshape(out_ref.shape)


def _post(attn, x, yb, yc, pd, ppre, pw_bd, ps_row, wo, fw_row, *, ns, tl, start_pos, final):
    b, l, d = x.shape
    n_tiles = l // tl
    seq3 = lambda w: pl.BlockSpec((ns, tl, w), lambda bi, ti: (bi, ti, 0))
    const2 = lambda a: pl.BlockSpec(a.shape, lambda bi, ti: (0, 0))
    if len(attn) == 1:
        attn_specs, slabs = [seq3(BRANCH)], []
    else:
        assert ns == 1

        def cls(dil):
            if dil == 1:
                return pl.BlockSpec((None, None, tl, BRANCH), lambda bi, ti: (bi, 0, ti, 0))
            return pl.BlockSpec((None, dil, tl // dil, BRANCH), lambda bi, ti: (bi, 0, ti, 0))

        attn_specs = 2 * [cls(dil) for _, dil in ATTN_PATTERNS] + [
            pl.BlockSpec((None, tl, BRANCH), lambda bi, ti: (bi, ti, 3))]
        slabs = [pltpu.VMEM((BRANCH // LANES, tl, LANES), F32)] * (2 * len(DILATED))
    return pl.pallas_call(
        functools.partial(_post_body, ns=ns, tl=tl, n_tiles=n_tiles, start_pos=start_pos, final=final,
                          n_attn=len(attn)),
        grid=(b // ns, n_tiles),
        in_specs=attn_specs + [seq3(d), seq3(BRANCH), seq3(BRANCH), seq3(2 * BRANCH),
                               pl.BlockSpec((ns, POOL_MAX, BRANCH), lambda bi, ti: (bi, 0, 0)),
                               const2(pw_bd), const2(ps_row), const2(wo), const2(fw_row)],
        out_specs=seq3(d),
        out_shape=jax.ShapeDtypeStruct((b, l, d), F32),
        scratch_shapes=[pltpu.VMEM((ns, POOL_MAX + tl, BRANCH), F32)] + slabs,
        compiler_params=pltpu.CompilerParams(dimension_semantics=("arbitrary", "arbitrary"),
                                             vmem_limit_bytes=VMEM_LIMIT),
        name="post",
    )(*attn, x, yb, yc, pd, ppre, pw_bd, ps_row, wo, fw_row)


def _prep_weights(w_in, w_out, pool_w):
    depth, d, _ = w_in.shape
    n_small = 2 * N_HEADS
    main = 8 * BRANCH
    w_perm = jnp.concatenate([
        w_in[:, :, 0:main], w_in[:, :, main + n_small:],
        w_in[:, :, main:main + n_small], jnp.zeros((depth, d, LANES - n_small), w_in.dtype)], axis=-1)
    groups = pool_w.shape[1]
    pw_bd = jnp.zeros((depth, BRANCH, BRANCH), pool_w.dtype)
    for g in range(groups):
        sl = slice(g * HEAD_DIM, (g + 1) * HEAD_DIM)
        pw_bd = pw_bd.at[:, sl, sl].set(pool_w[:, g])
    return w_perm.astype(BF16), w_out.astype(BF16), pw_bd.astype(BF16)


def _attn_prompt(pa3, classes, max_sub):
    b, l, w = pa3.shape
    os_, lses = [], []
    ci = 0
    for _, dil in ATTN_PATTERNS:
        n_sub = min(max_sub, l // dil // ATTN_STEPS)
        if dil == 1:
            o, lse = _swa(pa3.reshape(b, 1, l, w), 1, n_sub)
        else:
            o, lse = _swa(classes[ci], dil, n_sub)
            ci += 1
        os_.append(o)
        lses.append(lse)
    return os_ + lses + [pa3]


def _trunk(x, start_pos, states, prm, cfg):
    b, l, d = x.shape
    depth = prm['norm_w'].shape[0]
    dt = x.dtype
    ns, tl = cfg['ns'], cfg['tl']
    ks, vs, ds, dcs, hs, ps = [], [], [], [], [], []
    for layer in range(depth):
        if states is None:
            conv_prefix = jnp.zeros((b, CONV_TAPS - 1, 3 * BRANCH), dt)
            s_delta = jnp.zeros((b, N_HEADS, HEAD_DIM, HEAD_DIM), F32)
            s_hgrn = jnp.zeros((b, N_HEADS, HEAD_DIM, HEAD_DIM), F32)
            pool_prefix = jnp.zeros((b, POOL_MAX - 1, BRANCH), dt)
        else:
            c_kt, c_vt, c_delta, c_conv, c_hgrn, c_pool = states
            conv_prefix, s_delta, s_hgrn, pool_prefix = c_conv[layer], c_delta[layer], c_hgrn[layer], c_pool[layer]
        outs = _inproj(x.reshape(b * l, d), prm['norm_w'][layer][None, :], prm['w_in'][layer], cfg['tm'],
                       seq_len=l if states is None else None)
        pa, pb, pc, pd, pe = outs[0:5]
        pa3 = pa.reshape(b, l, 4 * BRANCH)
        a_k = pa3[:, :, BRANCH:2 * BRANCH]
        a_v = pa3[:, :, 2 * BRANCH:3 * BRANCH]
        if states is None:
            attn = _attn_prompt(pa3, outs[5:], cfg['swa_sub'])
            keep = min(ATTN_PATTERNS[-1][0], l)
            k_rows, v_rows = a_k[:, l - keep:], a_v[:, l - keep:]
        else:
            attn = [_attn_sample(pa3, c_kt, c_vt, layer, cfg['attn_nb'])]
            k_rows, v_rows = a_k, a_v
        k_rows = k_rows.reshape(k_rows.shape[:2] + (N_HEADS, HEAD_DIM))
        v_rows = v_rows.reshape(v_rows.shape[:2] + (N_HEADS, HEAD_DIM))

        pb3 = pb.reshape(b, l, 4 * BRANCH)
        cpre = jnp.pad(conv_prefix, ((0, 0), (SUBLANES - (CONV_TAPS - 1), 0), (0, 0)))
        yb, sd_new = _delta(pb3, pe.reshape(b, l, LANES), cpre, s_delta, prm['delta_conv_w'][layer],
                            prm['alog_row'][layer], prm['dtb_row'][layer], prm['delta_nw_row'][layer],
                            ns=ns, tl=tl, chunk=math.gcd(l, DELTA_CHUNK))
        conv_new = jnp.concatenate([conv_prefix, pb3[:, :, 0:3 * BRANCH]], axis=1)[:, -(CONV_TAPS - 1):]

        yc, sh_new_t = _hgrn(pc.reshape(b, l, 4 * BRANCH), prm['hgrn_lb_raw'], jnp.swapaxes(s_hgrn, -1, -2),
                             prm['hgrn_nw_row'][layer], ns=ns, tl=tl, chunk=math.gcd(l, HGRN_CHUNK), layer=layer)
        sh_new = jnp.swapaxes(sh_new_t, -1, -2)

        pd3 = pd.reshape(b, l, 2 * BRANCH)
        ppre = jnp.pad(pool_prefix, ((0, 0), (1, 0), (0, 0)))
        pool_new = jnp.concatenate([pool_prefix, pd3[:, :, 0:BRANCH]], axis=1)[:, -(POOL_MAX - 1):]
        x = _post(attn, x, yb, yc, pd3, ppre, prm['pool_w_bd'][layer], prm['pool_scale'][layer][None, :],
                  prm['w_out'][layer], prm['final_norm_w'][None, :], ns=cfg['ns_post'], tl=cfg['tl_post'],
                  start_pos=start_pos, final=(layer == depth - 1))
        ks.append(k_rows)
        vs.append(v_rows)
        ds.append(sd_new)
        dcs.append(conv_new)
        hs.append(sh_new)
        ps.append(pool_new)
    return x, (jnp.stack(ks), jnp.stack(vs), jnp.stack(ds), jnp.stack(dcs), jnp.stack(hs), jnp.stack(ps))


def _lane_row(vals, offset):
    depth, n = vals.shape
    return jnp.zeros((depth, 1, LANES), F32).at[:, 0, offset:offset + n].set(vals.astype(F32))


def kernel(x_prompt, x_sample, cache_attn_k, cache_attn_v, state_delta, state_delta_conv, state_hgrn, state_pool,
           norm_w, w_in, w_out, delta_conv_w, delta_a_log, delta_dt_bias, delta_norm_w, hgrn_lb_raw, hgrn_norm_w,
           pool_w, pool_scale, final_norm_w):
    w_in_p, w_out_b, pw_bd = _prep_weights(w_in, w_out, pool_w)
    prm = dict(
        norm_w=norm_w, w_in=w_in_p, w_out=w_out_b, pool_w_bd=pw_bd, pool_scale=pool_scale,
        final_norm_w=final_norm_w, delta_conv_w=delta_conv_w, hgrn_lb_raw=hgrn_lb_raw,
        alog_row=_lane_row(delta_a_log, N_HEADS), dtb_row=_lane_row(delta_dt_bias, N_HEADS),
        delta_nw_row=jnp.tile(delta_norm_w, (1, N_HEADS))[:, None, :],
        hgrn_nw_row=jnp.tile(hgrn_norm_w, (1, N_HEADS))[:, None, :],
    )
    bp = x_prompt.shape[0]
    y_p, st_p = _trunk(x_prompt, 0, None, prm, dict(tm=512, ns=bp, tl=256, ns_post=1, tl_post=512, swa_sub=4))
    dec_b, dec_l = x_sample.shape[0], x_sample.shape[1]
    depth, _, buf = cache_attn_k.shape[0:3]
    ckt = jnp.transpose(cache_attn_k, (0, 1, 3, 4, 2)).reshape(depth, dec_b, BRANCH, buf)
    cvt = jnp.transpose(cache_attn_v, (0, 1, 3, 4, 2)).reshape(depth, dec_b, BRANCH, buf)
    y_s, st_s = _trunk(x_sample, PAST_LEN, (ckt, cvt, state_delta, state_delta_conv, state_hgrn, state_pool), prm,
                       dict(tm=256, ns=32, tl=dec_l, ns_post=32, tl_post=dec_l, attn_nb=2))
    k_p, v_p, d_p, dc_p, h_p, pl_p = st_p
    k_s, v_s, d_s, dc_s, h_s, pl_s = st_s
    return (y_p, y_s, k_p, k_s, v_p, v_s, d_p, d_s, dc_p, dc_s, h_p, h_s, pl_p, pl_s)
```

```python
import functools
import math

import jax
import jax.numpy as jnp
from jax import lax
from jax.experimental import pallas as pl
from jax.experimental.pallas import tpu as pltpu

F32 = jnp.float32
BF16 = jnp.bfloat16

N_HEADS = 4
HEAD_DIM = 64
BRANCH = N_HEADS * HEAD_DIM
ATTN_PATTERNS = ((128, 1), (512, 4), (2048, 16))
ATTN_STEPS = 128
CONV_TAPS = 4
DELTA_CHUNK = 64
HGRN_CHUNK = 16
POOL_WINDOWS = (2, 4, 8, 16)
POOL_MAX = 16
PAST_LEN = 2048
NORM_EPS = 1e-6
NEG_MASK = -1e30
ALIBI_SLOPES = tuple(2.0 ** (-8.0 * (h + 1) / N_HEADS) for h in range(N_HEADS))

SUBLANES = 8
LANES = 128
VMEM_LIMIT = 56 * 1024 * 1024

SEG_A = (0, 4 * BRANCH)
SEG_B = (4 * BRANCH, 8 * BRANCH)
SEG_C = (8 * BRANCH, 12 * BRANCH)
SEG_D = (12 * BRANCH, 14 * BRANCH)
SEG_E = (14 * BRANCH, 14 * BRANCH + LANES)
IN_PAD = SEG_E[1]
DILATED = tuple(d for _, d in ATTN_PATTERNS if d > 1)


def _sigmoid(x):
    return 1.0 / (1.0 + jnp.exp(-x))


def _silu(x):
    return x * _sigmoid(x)


def _softplus(x):
    return jnp.maximum(x, 0.0) + jnp.log(1.0 + jnp.exp(-jnp.abs(x)))


def _head_ones():
    r = lax.broadcasted_iota(jnp.int32, (BRANCH, BRANCH), 0) // HEAD_DIM
    c = lax.broadcasted_iota(jnp.int32, (BRANCH, BRANCH), 1) // HEAD_DIM
    return jnp.where(r == c, 1.0, 0.0).astype(BF16)


def _split2(x):
    hi = x.astype(BF16)
    return hi, (x - hi.astype(F32)).astype(BF16)


def _split3(x):
    t1 = x.astype(BF16)
    r1 = x - t1.astype(F32)
    t2 = r1.astype(BF16)
    return t1, t2, (r1 - t2.astype(F32)).astype(BF16)


def _head_sum(x, g):
    hi, lo = _split2(x)
    return jnp.dot(hi, g, preferred_element_type=F32) + jnp.dot(lo, g, preferred_element_type=F32)


def _bmm(a, b):
    return jnp.einsum('bij,bjk->bik', a, b, preferred_element_type=F32)


def _bmm_nt(a, b):
    return jnp.einsum('bid,bjd->bij', a, b, preferred_element_type=F32)


def _bmm_tn(a, b):
    return jnp.einsum('bci,bcj->bij', a, b, preferred_element_type=F32)


def _bmm1(a, b):
    return _bmm(a.astype(BF16), b.astype(BF16))


def _bmm3(a, b):
    ah, al = _split2(a)
    bh, bl = _split2(b)
    return _bmm(ah, bh) + _bmm(ah, bl) + _bmm(al, bh)


def _to_head_batch(x2d, nb, chunk):
    parts = [x2d[:, h * HEAD_DIM:(h + 1) * HEAD_DIM].reshape(nb, 1, chunk, HEAD_DIM) for h in range(N_HEADS)]
    return jnp.concatenate(parts, axis=1).reshape(nb * N_HEADS, chunk, HEAD_DIM)


def _from_head_batch(x, nb, chunk):
    x4 = x.reshape(nb, N_HEADS, chunk, HEAD_DIM)
    return jnp.concatenate([x4[:, h].reshape(nb * chunk, HEAD_DIM) for h in range(N_HEADS)], axis=-1)


def _inproj_body(*refs, tm, prompt):
    x_ref, nw_ref, w_ref = refs[0:3]
    first_out = 5 if prompt else 3
    pa_ref, pb_ref, pc_ref, pd_ref, pe_ref = refs[first_out:first_out + 5]
    x = x_ref[...]
    ms = jnp.mean(x * x, axis=-1, keepdims=True)
    h = (x * lax.rsqrt(ms + NORM_EPS) * nw_ref[...]).astype(BF16)
    for ref, (lo, hi) in ((pb_ref, SEG_B), (pc_ref, SEG_C), (pd_ref, SEG_D), (pe_ref, SEG_E)):
        ref[...] = jnp.dot(h, w_ref[:, lo:hi], preferred_element_type=F32)
    p_a = jnp.dot(h, w_ref[:, SEG_A[0]:SEG_A[1]], preferred_element_type=F32)
    pa_ref[...] = p_a
    if not prompt:
        return
    class_refs = refs[first_out + 5:first_out + 5 + len(DILATED)]
    k_keep, v_keep, slab = refs[first_out + 5 + len(DILATED):]
    k_keep[...] = p_a[:, BRANCH:2 * BRANCH]
    v_keep[...] = p_a[:, 2 * BRANCH:3 * BRANCH]
    n_slab = 3 * BRANCH // LANES
    for c in range(n_slab):
        slab[c] = p_a[:, c * LANES:(c + 1) * LANES]
    for ref, dil in zip(class_refs, DILATED):
        for r in range(dil):
            for c in range(n_slab):
                ref[0, r, :, c * LANES:(c + 1) * LANES] = slab[c, pl.ds(r, tm // dil, stride=dil), :].astype(BF16)


def _inproj(x2d, nw, w, tm, seq_len=None, keep=None):
    t, d = x2d.shape
    widths = [s[1] - s[0] for s in (SEG_A, SEG_B, SEG_C, SEG_D, SEG_E)]
    out_specs = [pl.BlockSpec((tm, wd), lambda i: (i, 0)) for wd in widths]
    out_shape = [jax.ShapeDtypeStruct((t, wd), F32) for wd in widths]
    in_specs = [pl.BlockSpec((tm, d), lambda i: (i, 0)),
                pl.BlockSpec((1, d), lambda i: (0, 0)),
                pl.BlockSpec((d, IN_PAD), lambda i: (0, 0))]
    args = [x2d, nw, w]
    aliases, scratch = {}, []
    prompt = seq_len is not None
    if prompt:
        tps = seq_len // tm
        for dil in DILATED:
            out_specs.append(pl.BlockSpec((1, dil, tm // dil, 3 * BRANCH), lambda i: (i // tps, 0, i % tps, 0)))
            out_shape.append(jax.ShapeDtypeStruct((t // seq_len, dil, seq_len // dil, 3 * BRANCH), BF16))
        k_stack, v_stack, layer = keep
        skip = tps - k_stack.shape[2] // tm
        for stack in (k_stack, v_stack):
            aliases[len(args)] = len(out_shape)
            args.append(stack)
            in_specs.append(pl.BlockSpec(memory_space=pl.ANY))
            out_specs.append(pl.BlockSpec((None, None, tm, BRANCH),
                                          lambda i: (layer, i // tps, jnp.maximum(i % tps - skip, 0), 0)))
            out_shape.append(jax.ShapeDtypeStruct(stack.shape, stack.dtype))
        scratch = [pltpu.VMEM((3 * BRANCH // LANES, tm, LANES), F32)]
    return pl.pallas_call(
        functools.partial(_inproj_body, tm=tm, prompt=prompt),
        grid=(t // tm,),
        in_specs=in_specs,
        out_specs=out_specs,
        out_shape=out_shape,
        input_output_aliases=aliases,
        scratch_shapes=scratch,
        compiler_params=pltpu.CompilerParams(dimension_semantics=("arbitrary",), vmem_limit_bytes=VMEM_LIMIT),
        name="inproj",
    )(*args)


def _swa_body(q_ref, kp_ref, kc_ref, vp_ref, vc_ref, o_ref, lse_ref, *, dil, tq, n_sub):
    n = pl.program_id(1)
    q = (q_ref[...].astype(F32) * (HEAD_DIM ** -0.5)).astype(BF16)
    k = jnp.concatenate([kp_ref[...], kc_ref[...]], axis=0).astype(BF16)
    v = jnp.concatenate([vp_ref[...], vc_ref[...]], axis=0).astype(BF16)
    iq = lax.broadcasted_iota(jnp.int32, (tq, 2 * tq), 0)
    jk = lax.broadcasted_iota(jnp.int32, (tq, 2 * tq), 1)
    dist = iq + tq - jk
    in_band = (dist >= 0) & (dist <= ATTN_STEPS)
    bias = (dist * dil).astype(F32)
    head_bias = [ALIBI_SLOPES[h] * bias for h in range(N_HEADS)]
    first_valid = in_band & ((jk >= tq) | (n > 0))
    ones = jnp.ones((v.shape[0], HEAD_DIM), BF16)
    v_aug = [jnp.concatenate([v[:, h * HEAD_DIM:(h + 1) * HEAD_DIM], ones], axis=-1) for h in range(N_HEADS)]
    for i in range(n_sub):
        valid = first_valid if i == 0 else in_band
        rows = slice(i * tq, (i + 1) * tq)
        keys = slice(i * tq, (i + 2) * tq)
        outs, lses = [], []
        for h in range(N_HEADS):
            sl = slice(h * HEAD_DIM, (h + 1) * HEAD_DIM)
            s = lax.dot_general(q[rows, sl], k[keys, sl], (((1,), (1,)), ((), ())), preferred_element_type=F32)
            s = jnp.where(valid, s - head_bias[h], NEG_MASK)
            m = jnp.max(s, axis=-1, keepdims=True)
            p = jnp.exp(s - m)
            nd = jnp.dot(p.astype(BF16), v_aug[h][keys], preferred_element_type=F32)
            den = nd[:, HEAD_DIM:2 * HEAD_DIM]
            outs.append(nd[:, 0:HEAD_DIM] / den)
            lses.append(m + jnp.log(den))
        o_ref[rows, :] = jnp.concatenate(outs, axis=-1)
        lse_ref[rows, :] = jnp.concatenate(lses, axis=-1)


def _swa(qkv, dil, n_sub):
    b, _, n, _ = qkv.shape
    tq = ATTN_STEPS
    tqb = tq * n_sub

    def cur(col):
        return pl.BlockSpec((None, None, tqb, BRANCH), lambda zi, ni: (zi // dil, zi % dil, ni, col))

    def prev(col):
        return pl.BlockSpec((None, None, tq, BRANCH),
                            lambda zi, ni: (zi // dil, zi % dil, jnp.maximum(ni * n_sub - 1, 0), col))

    shp = jax.ShapeDtypeStruct((b, dil, n, BRANCH), F32)
    return pl.pallas_call(
        functools.partial(_swa_body, dil=dil, tq=tq, n_sub=n_sub),
        grid=(b * dil, n // tqb),
        in_specs=[cur(0), prev(1), cur(1), prev(2), cur(2)],
        out_specs=[cur(0), cur(0)],
        out_shape=[shp, shp],
        compiler_params=pltpu.CompilerParams(dimension_semantics=("arbitrary", "arbitrary"),
                                             vmem_limit_bytes=VMEM_LIMIT),
        name=f"swa_d{dil}",
    )(qkv, qkv, qkv, qkv, qkv)


def _merge_patterns(o_refs, lse_refs, scratch, tm):
    outs, lses = [], []
    si = 0
    for (_, dil), o_ref, lse_ref in zip(ATTN_PATTERNS, o_refs, lse_refs):
        if dil == 1:
            outs.append(o_ref[...])
            lses.append(lse_ref[...])
            continue
        ob, lb = scratch[si], scratch[si + 1]
        si += 2
        n_slab = BRANCH // LANES
        for r in range(dil):
            for c in range(n_slab):
                ob[c, pl.ds(r, tm // dil, stride=dil), :] = o_ref[r, :, c * LANES:(c + 1) * LANES]
                lb[c, pl.ds(r, tm // dil, stride=dil), :] = lse_ref[r, :, c * LANES:(c + 1) * LANES]
        outs.append(jnp.concatenate([ob[c] for c in range(n_slab)], axis=-1))
        lses.append(jnp.concatenate([lb[c] for c in range(n_slab)], axis=-1))
    l_all = lses[0]
    for l in lses[1:]:
        l_all = jnp.maximum(l_all, l)
    numer = jnp.zeros_like(l_all)
    denom = jnp.zeros_like(l_all)
    for o, l in zip(outs, lses):
        wgt = jnp.exp(l - l_all)
        numer = numer + wgt * o
        denom = denom + wgt
    return numer / denom


def _attn_sample_one(pa, kt, vt, t_new, buf):
    rows = N_HEADS * t_new
    q = pa[:, 0:BRANCH] * (HEAD_DIM ** -0.5)
    k_new = pa[:, BRANCH:2 * BRANCH]
    v_new = pa[:, 2 * BRANCH:3 * BRANCH]
    gate = pa[:, 3 * BRANCH:4 * BRANCH]
    row_h = lax.broadcasted_iota(jnp.int32, (rows, BRANCH), 0) // t_new
    lane_h = lax.broadcasted_iota(jnp.int32, (rows, BRANCH), 1) // HEAD_DIM
    own = row_h == lane_h
    q_bd = jnp.where(own, jnp.concatenate([q] * N_HEADS, axis=0), 0.0).astype(BF16)
    pad = jnp.zeros((LANES - t_new, BRANCH), F32)
    k_new_p = jnp.concatenate([k_new, pad], axis=0).astype(BF16)
    v_new_p = jnp.concatenate([v_new, pad], axis=0).astype(BF16)
    kt = kt.astype(BF16)
    vt = vt.astype(BF16)
    nt = (((1,), (1,)), ((), ()))
    s = jnp.concatenate([jnp.dot(q_bd, kt, preferred_element_type=F32),
                         lax.dot_general(q_bd, k_new_p, nt, preferred_element_type=F32)], axis=-1)
    ncol = buf + LANES
    col = lax.broadcasted_iota(jnp.int32, (rows, ncol), 1)
    row = lax.broadcasted_iota(jnp.int32, (rows, ncol), 0)
    delta = buf + row % t_new - col
    hrow = row // t_new
    slope = jnp.where(hrow == 0, ALIBI_SLOPES[0],
                      jnp.where(hrow == 1, ALIBI_SLOPES[1], jnp.where(hrow == 2, ALIBI_SLOPES[2], ALIBI_SLOPES[3])))
    s = s - slope * delta.astype(F32)
    ps, ms, dens = [], [], []
    for win, dil in ATTN_PATTERNS:
        valid = (delta >= 0) & (delta <= win) & ((delta & (dil - 1)) == 0)
        sp = jnp.where(valid, s, NEG_MASK)
        m = jnp.max(sp, axis=-1, keepdims=True)
        p = jnp.exp(sp - m)
        ps.append(p.astype(BF16))
        ms.append(m)
        dens.append(jnp.sum(p, axis=-1, keepdims=True))
    p_all = jnp.concatenate(ps, axis=0)
    num_all = (lax.dot_general(p_all[:, :buf], vt, nt, preferred_element_type=F32)
               + jnp.dot(p_all[:, buf:], v_new_p, preferred_element_type=F32))
    m_all = jnp.maximum(jnp.maximum(ms[0], ms[1]), ms[2])
    numer = jnp.zeros((rows, BRANCH), F32)
    denom = jnp.zeros((rows, 1), F32)
    for g in range(len(ATTN_PATTERNS)):
        wgt = jnp.exp(ms[g] - m_all)
        numer = numer + wgt * num_all[g * rows:(g + 1) * rows]
        denom = denom + wgt * dens[g]
    o_full = jnp.where(own, numer / denom, 0.0)
    o = o_full[0:t_new]
    for h in range(1, N_HEADS):
        o = o + o_full[h * t_new:(h + 1) * t_new]
    return o * _silu(gate)


def _attn_sample_body(pa_ref, kt_ref, vt_ref, out_ref, *, t_new, buf, nb):
    for j in range(nb):
        out_ref[j] = _attn_sample_one(pa_ref[j], kt_ref[j], vt_ref[j], t_new, buf)


def _attn_sample(pa, cache_kt, cache_vt, layer, nb):
    b, t_new, _ = pa.shape
    buf = cache_kt.shape[3]
    cache = pl.BlockSpec((None, nb, BRANCH, buf), lambda i: (layer, i, 0, 0))
    return pl.pallas_call(
        functools.partial(_attn_sample_body, t_new=t_new, buf=buf, nb=nb),
        grid=(b // nb,),
        in_specs=[pl.BlockSpec((nb, t_new, 4 * BRANCH), lambda i: (i, 0, 0)), cache, cache],
        out_specs=pl.BlockSpec((nb, t_new, BRANCH), lambda i: (i, 0, 0)),
        out_shape=jax.ShapeDtypeStruct((b, t_new, BRANCH), F32),
        compiler_params=pltpu.CompilerParams(dimension_semantics=("arbitrary",), vmem_limit_bytes=VMEM_LIMIT),
        name="attn_sample",
    )(pa, cache_kt, cache_vt)


def _delta_body(pb_ref, pe_ref, cpre_ref, s0_ref, cw_ref, alog_ref, dtb_ref, nw_ref, _stack_ref, y_ref, s_out_ref,
                xbuf, s_scr, *, ns, tl, chunk, n_tiles):
    t = pl.program_id(1)
    nc = tl // chunk
    nb = ns * nc
    nbh = nb * N_HEADS
    rows = ns * tl
    width = 3 * BRANCH

    @pl.when(t == 0)
    def _():
        xbuf[:, 0:SUBLANES, :] = cpre_ref[...]
        s_scr[...] = s0_ref[...]

    xbuf[:, SUBLANES:SUBLANES + tl, :] = pb_ref[:, :, 0:width]
    conv = jnp.zeros((ns, tl, width), F32)
    for tap in range(CONV_TAPS):
        off = SUBLANES - (CONV_TAPS - 1) + tap
        conv = conv + xbuf[:, off:off + tl, :] * cw_ref[tap:tap + 1, :]
    if n_tiles > 1:
        xbuf[:, 0:SUBLANES, :] = xbuf[:, tl:tl + SUBLANES, :]
    conv = _silu(conv).reshape(rows, width)
    g_ones = _head_ones()
    q = conv[:, 0:BRANCH]
    k = conv[:, BRANCH:2 * BRANCH]
    v = conv[:, 2 * BRANCH:3 * BRANCH]
    q = q * lax.rsqrt(_head_sum(q * q, g_ones) + 1e-6) * (HEAD_DIM ** -0.5)
    k = k * lax.rsqrt(_head_sum(k * k, g_ones) + 1e-6)
    pe = pe_ref[...].reshape(rows, LANES)
    lane = lax.broadcasted_iota(jnp.int32, (rows, LANES), 1)
    gb = jnp.where(lane < N_HEADS, _sigmoid(pe), -jnp.exp(alog_ref[...]) * _softplus(pe + dtb_ref[...]))

    qh = _to_head_batch(q, nb, chunk)
    kh = _to_head_batch(k, nb, chunk)
    vh = _to_head_batch(v, nb, chunk)

    def col_batch(first):
        cols = [gb[:, first + h:first + h + 1].reshape(nb, 1, chunk, 1) for h in range(N_HEADS)]
        return jnp.concatenate(cols, axis=1).reshape(nbh, chunk, 1)

    beta = col_batch(0)
    g = col_batch(N_HEADS)
    ri = lax.broadcasted_iota(jnp.int32, (nbh, chunk, chunk), 1)
    ci = lax.broadcasted_iota(jnp.int32, (nbh, chunk, chunk), 2)
    ltri = jnp.where(ri >= ci, 1.0, 0.0).astype(BF16)
    eye = jnp.where(ri == ci, 1.0, 0.0).astype(F32)
    gm = jnp.where(ri > ci, jnp.broadcast_to(g, (nbh, chunk, chunk)), 0.0)
    dmat = sum(_bmm(ltri, term) for term in _split3(gm))
    gc = dmat[:, :, 0:1] + g[:, 0:1, :]
    g_last = gc[:, chunk - 1:chunk, :]
    decay = jnp.where(ri >= ci, jnp.exp(dmat), 0.0)
    kb = kh * beta
    vb = vh * beta
    kh_b = kh.astype(BF16)
    a_mat = jnp.where(ri > ci, _bmm_nt(kb.astype(BF16), kh_b) * decay, 0.0)
    n_fac = int(math.log2(chunk))
    x = -a_mat
    u = eye + x
    x = _bmm1(x, x)
    for _ in range(2, n_fac):
        prod = _bmm1(jnp.concatenate([x, u], axis=1), x)
        x = prod[:, 0:chunk]
        u = u + prod[:, chunk:2 * chunk]
    t_inv = u + _bmm1(u, x)
    uw = _bmm3(t_inv, jnp.concatenate([vb, kb * jnp.exp(gc)], axis=-1))
    u_mat = uw[:, :, 0:HEAD_DIM]
    w_mat = uw[:, :, HEAD_DIM:2 * HEAD_DIM]
    qk = _bmm_nt(qh.astype(BF16), kh_b) * decay
    qg = qh * jnp.exp(gc)
    kg = kh * jnp.exp(g_last - gc)
    e_last = jnp.exp(g_last)

    def sel(a, c):
        a5 = a.reshape((ns, nc, N_HEADS) + a.shape[1:])
        return a5[:, c].reshape((ns * N_HEADS,) + a.shape[1:])

    st = s_scr[...].reshape(ns * N_HEADS, HEAD_DIM, HEAD_DIM)
    o_chunks = []
    for c in range(nc):
        s_b = st.astype(BF16)
        wq = _bmm(jnp.concatenate([sel(w_mat, c), sel(qg, c)], axis=1).astype(BF16), s_b)
        v_new = (sel(u_mat, c) - wq[:, 0:chunk]).astype(BF16)
        o_chunks.append(wq[:, chunk:2 * chunk] + _bmm(sel(qk, c).astype(BF16), v_new))
        st = st * sel(e_last, c) + _bmm_tn(sel(kg, c).astype(BF16), v_new)
    s_scr[...] = st.reshape(ns, N_HEADS, HEAD_DIM, HEAD_DIM)
    if nc > 1:
        o_all = jnp.concatenate([oc.reshape(ns, 1, N_HEADS, chunk, HEAD_DIM) for oc in o_chunks], axis=1)
        o_all = o_all.reshape(nbh, chunk, HEAD_DIM)
    else:
        o_all = o_chunks[0]
    o = _from_head_batch(o_all, nb, chunk)
    o = o * lax.rsqrt(_head_sum(o * o, g_ones) * (1.0 / HEAD_DIM) + NORM_EPS) * nw_ref[...]
    gate = pb_ref[:, :, width:width + BRANCH].reshape(rows, BRANCH)
    y_ref[...] = (o * _silu(gate)).reshape(ns, tl, BRANCH)

    @pl.when(t == n_tiles - 1)
    def _():
        s_out_ref[...] = s_scr[...]


def _state_spec(ns, layer):
    return pl.BlockSpec((None, ns, N_HEADS, HEAD_DIM, HEAD_DIM), lambda bi, ti: (layer, bi, 0, 0, 0))


def _delta(pb, pe, cpre, s0, s0_layer, s_stack, layer, cw, alog_row, dtb_row, nw_row, *, ns, tl, chunk):
    b, l, _ = pb.shape
    n_tiles = l // tl
    seq3 = lambda w: pl.BlockSpec((ns, tl, w), lambda bi, ti: (bi, ti, 0))
    const2 = lambda a: pl.BlockSpec(a.shape, lambda bi, ti: (0, 0))
    return pl.pallas_call(
        functools.partial(_delta_body, ns=ns, tl=tl, chunk=chunk, n_tiles=n_tiles),
        grid=(b // ns, n_tiles),
        in_specs=[seq3(4 * BRANCH), seq3(LANES),
                  pl.BlockSpec((ns, SUBLANES, 3 * BRANCH), lambda bi, ti: (bi, 0, 0)), _state_spec(ns, s0_layer),
                  const2(cw), const2(alog_row), const2(dtb_row), const2(nw_row),
                  pl.BlockSpec(memory_space=pl.ANY)],
        out_specs=[seq3(BRANCH), _state_spec(ns, layer)],
        out_shape=[jax.ShapeDtypeStruct((b, l, BRANCH), F32),
                   jax.ShapeDtypeStruct(s_stack.shape, F32)],
        input_output_aliases={8: 1},
        scratch_shapes=[pltpu.VMEM((ns, SUBLANES + tl, 3 * BRANCH), F32),
                        pltpu.VMEM((ns, N_HEADS, HEAD_DIM, HEAD_DIM), F32)],
        compiler_params=pltpu.CompilerParams(dimension_semantics=("arbitrary", "arbitrary"),
                                             vmem_limit_bytes=VMEM_LIMIT),
        name="delta",
    )(pb, pe, cpre, s0, cw, alog_row, dtb_row, nw_row, s_stack)


def _hgrn_body(pc_ref, lbraw_ref, s0_ref, nw_ref, _stack_ref, y_ref, s_out_ref, s_scr,
               *, ns, tl, chunk, n_tiles, layer):
    t = pl.program_id(1)
    nc = tl // chunk
    nb = ns * nc
    rows = ns * tl

    def flip(s4):
        return jnp.swapaxes(s4.reshape(ns * N_HEADS, HEAD_DIM, HEAD_DIM), 1, 2).reshape(s4.shape)

    @pl.when(t == 0)
    def _():
        s_scr[...] = flip(s0_ref[...])

    raw = lbraw_ref[...]
    e = jnp.exp(raw - jnp.max(raw, axis=0, keepdims=True))
    sm = e / jnp.sum(e, axis=0, keepdims=True)
    lb = jnp.zeros((1, BRANCH), F32)
    for d in range(1, layer + 1):
        lb = lb + sm[d:d + 1, :]

    pc = pc_ref[...].reshape(rows, 4 * BRANCH)
    qh = _silu(pc[:, 0:BRANCH])
    fr = pc[:, BRANCH:2 * BRANCH]
    vh = pc[:, 2 * BRANCH:3 * BRANCH]
    gate = pc[:, 3 * BRANCH:4 * BRANCH]
    f = lb + (1.0 - lb) * _sigmoid(fr)
    log_f = jnp.log(f).reshape(nb, chunk, BRANCH)
    kh = (1.0 - lb) * _sigmoid(-fr)

    q3 = qh.reshape(nb, chunk, BRANCH)
    k3 = kh.reshape(nb, chunk, BRANCH)
    v3 = vh.reshape(nb, chunk, BRANCH)
    ri = lax.broadcasted_iota(jnp.int32, (nb, chunk, chunk), 1)
    ci = lax.broadcasted_iota(jnp.int32, (nb, chunk, chunk), 2)
    ltri = jnp.where(ri >= ci, 1.0, 0.0).astype(BF16)
    l1 = log_f.astype(BF16)
    r1 = log_f - l1.astype(F32)
    l2 = r1.astype(BF16)
    l3 = (r1 - l2.astype(F32)).astype(BF16)
    gcum = _bmm(ltri, l1) + _bmm(ltri, l2) + _bmm(ltri, l3)
    g_last = gcum[:, chunk - 1:chunk, :]
    qg = q3 * jnp.exp(gcum)
    kg = k3 * jnp.exp(g_last - gcum)
    e_last = jnp.exp(g_last)

    g_ones = _head_ones()
    n_grp = chunk // SUBLANES
    row_i = [g * SUBLANES + lax.broadcasted_iota(jnp.int32, (nb, chunk - g * SUBLANES, BRANCH), 1)
             for g in range(n_grp)]
    acc = [jnp.zeros((nb, SUBLANES, BRANCH), F32) for _ in range(n_grp)]
    for j in range(chunk):
        g0 = j // SUBLANES
        lo = g0 * SUBLANES
        dec = jnp.exp(jnp.where(row_i[g0] >= j, gcum[:, lo:, :] - gcum[:, j:j + 1, :], NEG_MASK))
        t_j = (q3[:, lo:, :] * dec * k3[:, j:j + 1, :]).reshape(nb * (chunk - lo), BRANCH)
        a_j = jnp.dot(t_j.astype(BF16), g_ones, preferred_element_type=F32).reshape(nb, chunk - lo, BRANCH)
        c_j = a_j * v3[:, j:j + 1, :]
        for gi in range(g0, n_grp):
            acc[gi] = acc[gi] + c_j[:, (gi - g0) * SUBLANES:(gi - g0 + 1) * SUBLANES, :]
    intra = jnp.concatenate(acc, axis=1) if n_grp > 1 else acc[0]

    qg_b = _to_head_batch(qg.reshape(rows, BRANCH), nb, chunk).astype(BF16)
    kg_b = _to_head_batch(kg.reshape(rows, BRANCH), nb, chunk).astype(BF16)
    v_b = _to_head_batch(vh, nb, chunk).astype(BF16)
    el_b = _to_head_batch(e_last.reshape(nb, BRANCH), nb, 1)

    def sel(a, c):
        a5 = a.reshape((ns, nc, N_HEADS) + a.shape[1:])
        return a5[:, c].reshape((ns * N_HEADS,) + a.shape[1:])

    st = s_scr[...].reshape(ns * N_HEADS, HEAD_DIM, HEAD_DIM)
    o_chunks = []
    for c in range(nc):
        o_chunks.append(_bmm_nt(sel(qg_b, c), st.astype(BF16)))
        st = st * sel(el_b, c) + _bmm_tn(sel(v_b, c), sel(kg_b, c))
    s_scr[...] = st.reshape(ns, N_HEADS, HEAD_DIM, HEAD_DIM)
    if nc > 1:
        o_all = jnp.concatenate([oc.reshape(ns, 1, N_HEADS, chunk, HEAD_DIM) for oc in o_chunks], axis=1)
        o_all = o_all.reshape(nb * N_HEADS, chunk, HEAD_DIM)
    else:
        o_all = o_chunks[0]
    o = _from_head_batch(o_all, nb, chunk) + intra.reshape(rows, BRANCH)
    o = o * lax.rsqrt(_head_sum(o * o, g_ones) * (1.0 / HEAD_DIM) + NORM_EPS) * nw_ref[...]
    y_ref[...] = (o * _silu(gate)).reshape(ns, tl, BRANCH)

    @pl.when(t == n_tiles - 1)
    def _():
        s_out_ref[...] = flip(s_scr[...])


def _hgrn(pc, lb_raw, s0, s0_layer, s_stack, nw_row, *, ns, tl, chunk, layer):
    b, l, _ = pc.shape
    n_tiles = l // tl
    return pl.pallas_call(
        functools.partial(_hgrn_body, ns=ns, tl=tl, chunk=chunk, n_tiles=n_tiles, layer=layer),
        grid=(b // ns, n_tiles),
        in_specs=[pl.BlockSpec((ns, tl, 4 * BRANCH), lambda bi, ti: (bi, ti, 0)),
                  pl.BlockSpec(lb_raw.shape, lambda bi, ti: (0, 0)), _state_spec(ns, s0_layer),
                  pl.BlockSpec(nw_row.shape, lambda bi, ti: (0, 0)),
                  pl.BlockSpec(memory_space=pl.ANY)],
        out_specs=[pl.BlockSpec((ns, tl, BRANCH), lambda bi, ti: (bi, ti, 0)), _state_spec(ns, layer)],
        out_shape=[jax.ShapeDtypeStruct((b, l, BRANCH), F32),
                   jax.ShapeDtypeStruct(s_stack.shape, F32)],
        input_output_aliases={4: 1},
        scratch_shapes=[pltpu.VMEM((ns, N_HEADS, HEAD_DIM, HEAD_DIM), F32)],
        compiler_params=pltpu.CompilerParams(dimension_semantics=("arbitrary", "arbitrary"),
                                             vmem_limit_bytes=VMEM_LIMIT),
        name="hgrn",
    )(pc, lb_raw, s0, nw_row, s_stack)


def _post_body(*refs, ns, tl, n_tiles, start_pos, final, n_attn):
    attn_refs = refs[0:n_attn]
    (x_ref, yb_ref, yc_ref, pd_ref, ppre_ref, pw_ref, ps_ref, wo_ref, fw_ref, out_ref, xbuf) = refs[n_attn:n_attn + 11]
    slabs = refs[n_attn + 11:]
    t = pl.program_id(1)
    rows = ns * tl
    if n_attn == 1:
        ya = attn_refs[0][...].reshape(rows, BRANCH)
    else:
        n_g = len(ATTN_PATTERNS)
        ya = _merge_patterns(attn_refs[0:n_g], attn_refs[n_g:2 * n_g], slabs, tl) * _silu(attn_refs[2 * n_g][...])

    @pl.when(t == 0)
    def _():
        xbuf[:, 0:POOL_MAX, :] = ppre_ref[...]

    xbuf[:, POOL_MAX:POOL_MAX + tl, :] = pd_ref[:, :, 0:BRANCH]

    def back(kk):
        return xbuf[:, POOL_MAX - kk:POOL_MAX - kk + tl, :]

    x0 = back(0)
    sums = {}
    acc = x0
    for kk in range(1, POOL_MAX):
        acc = acc + back(kk)
        if kk + 1 in POOL_WINDOWS:
            sums[kk + 1] = acc
    if n_tiles > 1:
        xbuf[:, 0:POOL_MAX, :] = xbuf[:, tl:tl + POOL_MAX, :]
    group = lax.broadcasted_iota(jnp.int32, (ns, tl, BRANCH), 2) // HEAD_DIM
    pos = start_pos + t * tl + lax.broadcasted_iota(jnp.int32, (ns, tl, BRANCH), 1)
    tot = sums[POOL_WINDOWS[-1]]
    win = jnp.full((ns, tl, BRANCH), POOL_WINDOWS[-1], jnp.int32)
    for gi in range(len(POOL_WINDOWS) - 2, -1, -1):
        tot = jnp.where(group == gi, sums[POOL_WINDOWS[gi]], tot)
        win = jnp.where(group == gi, POOL_WINDOWS[gi], win)
    cnt = jnp.minimum(pos + 1, win).astype(F32)
    pooled = (tot / cnt - x0).reshape(rows, BRANCH)
    gate_d = pd_ref[:, :, BRANCH:2 * BRANCH].reshape(rows, BRANCH)
    yd = jnp.dot(pooled.astype(BF16), pw_ref[...], preferred_element_type=F32) * ps_ref[...] * _silu(gate_d)
    ycat = jnp.concatenate([ya, yb_ref[...].reshape(rows, BRANCH),
                            yc_ref[...].reshape(rows, BRANCH), yd], axis=-1).astype(BF16)
    x_new = x_ref[...].reshape(rows, -1) + jnp.dot(ycat, wo_ref[...], preferred_element_type=F32)
    if final:
        ms = jnp.mean(x_new * x_new, axis=-1, keepdims=True)
        x_new = x_new * lax.rsqrt(ms + NORM_EPS) * fw_ref[...]
    out_ref[...] = x_new.reshape(out_ref.shape)


def _post(attn, x, yb, yc, pd, ppre, pw_bd, ps_row, wo, fw_row, *, ns, tl, start_pos, final):
    b, l, d = x.shape
    n_tiles = l // tl
    seq3 = lambda w: pl.BlockSpec((ns, tl, w), lambda bi, ti: (bi, ti, 0))
    const2 = lambda a: pl.BlockSpec(a.shape, lambda bi, ti: (0, 0))
    if len(attn) == 1:
        attn_specs, slabs = [seq3(BRANCH)], []
    else:
        assert ns == 1

        def cls(dil):
            if dil == 1:
                return pl.BlockSpec((None, None, tl, BRANCH), lambda bi, ti: (bi, 0, ti, 0))
            return pl.BlockSpec((None, dil, tl // dil, BRANCH), lambda bi, ti: (bi, 0, ti, 0))

        attn_specs = 2 * [cls(dil) for _, dil in ATTN_PATTERNS] + [
            pl.BlockSpec((None, tl, BRANCH), lambda bi, ti: (bi, ti, 3))]
        slabs = [pltpu.VMEM((BRANCH // LANES, tl, LANES), F32)] * (2 * len(DILATED))
    return pl.pallas_call(
        functools.partial(_post_body, ns=ns, tl=tl, n_tiles=n_tiles, start_pos=start_pos, final=final,
                          n_attn=len(attn)),
        grid=(b // ns, n_tiles),
        in_specs=attn_specs + [seq3(d), seq3(BRANCH), seq3(BRANCH), seq3(2 * BRANCH),
                               pl.BlockSpec((ns, POOL_MAX, BRANCH), lambda bi, ti: (bi, 0, 0)),
                               const2(pw_bd), const2(ps_row), const2(wo), const2(fw_row)],
        out_specs=seq3(d),
        out_shape=jax.ShapeDtypeStruct((b, l, d), F32),
        scratch_shapes=[pltpu.VMEM((ns, POOL_MAX + tl, BRANCH), F32)] + slabs,
        compiler_params=pltpu.CompilerParams(dimension_semantics=("arbitrary", "arbitrary"),
                                             vmem_limit_bytes=VMEM_LIMIT),
        name="post",
    )(*attn, x, yb, yc, pd, ppre, pw_bd, ps_row, wo, fw_row)


def _prep_weights(w_in, w_out, pool_w):
    depth, d, _ = w_in.shape
    n_small = 2 * N_HEADS
    main = 8 * BRANCH
    w_perm = jnp.concatenate([
        w_in[:, :, 0:main], w_in[:, :, main + n_small:],
        w_in[:, :, main:main + n_small], jnp.zeros((depth, d, LANES - n_small), w_in.dtype)], axis=-1)
    groups = pool_w.shape[1]
    pw_bd = jnp.zeros((depth, BRANCH, BRANCH), pool_w.dtype)
    for g in range(groups):
        sl = slice(g * HEAD_DIM, (g + 1) * HEAD_DIM)
        pw_bd = pw_bd.at[:, sl, sl].set(pool_w[:, g])
    return w_perm.astype(BF16), w_out.astype(BF16), pw_bd.astype(BF16)


def _attn_prompt(pa3, classes, max_sub):
    b, l, w = pa3.shape
    os_, lses = [], []
    ci = 0
    for _, dil in ATTN_PATTERNS:
        n_sub = min(max_sub, l // dil // ATTN_STEPS)
        if dil == 1:
            o, lse = _swa(pa3.reshape(b, 1, l, w), 1, n_sub)
        else:
            o, lse = _swa(classes[ci], dil, n_sub)
            ci += 1
        os_.append(o)
        lses.append(lse)
    return os_ + lses + [pa3]


def _trunk(x, start_pos, states, prm, cfg):
    b, l, d = x.shape
    depth = prm['norm_w'].shape[0]
    dt = x.dtype
    ns, tl = cfg['ns'], cfg['tl']
    prompt = states is None
    state_shape = (depth, b, N_HEADS, HEAD_DIM, HEAD_DIM)
    d_stack = jnp.zeros(state_shape, F32)
    h_stack = jnp.zeros(state_shape, F32)
    if prompt:
        keep = min(ATTN_PATTERNS[-1][0], l)
        k_stack = jnp.zeros((depth, b, keep, BRANCH), F32)
        v_stack = jnp.zeros((depth, b, keep, BRANCH), F32)
        d_in = h_in = jnp.zeros((1,) + state_shape[1:], F32)
    else:
        c_kt, c_vt, d_in, c_conv, h_in, c_pool = states
    ks, vs, dcs, ps = [], [], [], []
    for layer in range(depth):
        if prompt:
            conv_prefix = jnp.zeros((b, CONV_TAPS - 1, 3 * BRANCH), dt)
            pool_prefix = jnp.zeros((b, POOL_MAX - 1, BRANCH), dt)
        else:
            conv_prefix, pool_prefix = c_conv[layer], c_pool[layer]
        outs = _inproj(x.reshape(b * l, d), prm['norm_w'][layer][None, :], prm['w_in'][layer], cfg['tm'],
                       seq_len=l if prompt else None, keep=(k_stack, v_stack, layer) if prompt else None)
        pa, pb, pc, pd, pe = outs[0:5]
        pa3 = pa.reshape(b, l, 4 * BRANCH)
        if prompt:
            n_cls = len(DILATED)
            attn = _attn_prompt(pa3, outs[5:5 + n_cls], cfg['swa_sub'])
            k_stack, v_stack = outs[5 + n_cls:]
        else:
            attn = [_attn_sample(pa3, c_kt, c_vt, layer, cfg['attn_nb'])]
            ks.append(pa3[:, :, BRANCH:2 * BRANCH].reshape(b, l, N_HEADS, HEAD_DIM))
            vs.append(pa3[:, :, 2 * BRANCH:3 * BRANCH].reshape(b, l, N_HEADS, HEAD_DIM))

        s0_layer = 0 if prompt else layer
        pb3 = pb.reshape(b, l, 4 * BRANCH)
        cpre = jnp.pad(conv_prefix, ((0, 0), (SUBLANES - (CONV_TAPS - 1), 0), (0, 0)))
        yb, d_stack = _delta(pb3, pe.reshape(b, l, LANES), cpre, d_in, s0_layer, d_stack, layer,
                             prm['delta_conv_w'][layer], prm['alog_row'][layer], prm['dtb_row'][layer],
                             prm['delta_nw_row'][layer], ns=ns, tl=tl, chunk=math.gcd(l, DELTA_CHUNK))
        dcs.append(jnp.concatenate([conv_prefix, pb3[:, :, 0:3 * BRANCH]], axis=1)[:, -(CONV_TAPS - 1):])

        yc, h_stack = _hgrn(pc.reshape(b, l, 4 * BRANCH), prm['hgrn_lb_raw'], h_in, s0_layer, h_stack,
                            prm['hgrn_nw_row'][layer], ns=ns, tl=tl, chunk=math.gcd(l, HGRN_CHUNK), layer=layer)

        pd3 = pd.reshape(b, l, 2 * BRANCH)
        ppre = jnp.pad(pool_prefix, ((0, 0), (1, 0), (0, 0)))
        ps.append(jnp.concatenate([pool_prefix, pd3[:, :, 0:BRANCH]], axis=1)[:, -(POOL_MAX - 1):])
        x = _post(attn, x, yb, yc, pd3, ppre, prm['pool_w_bd'][layer], prm['pool_scale'][layer][None, :],
                  prm['w_out'][layer], prm['final_norm_w'][None, :], ns=cfg['ns_post'], tl=cfg['tl_post'],
                  start_pos=start_pos, final=(layer == depth - 1))
    if prompt:
        k_all = k_stack.reshape(depth, b, keep, N_HEADS, HEAD_DIM)
        v_all = v_stack.reshape(depth, b, keep, N_HEADS, HEAD_DIM)
    else:
        k_all, v_all = jnp.stack(ks), jnp.stack(vs)
    return x, (k_all, v_all, d_stack, jnp.stack(dcs), h_stack, jnp.stack(ps))


def _lane_row(vals, offset):
    depth, n = vals.shape
    return jnp.zeros((depth, 1, LANES), F32).at[:, 0, offset:offset + n].set(vals.astype(F32))


def kernel(x_prompt, x_sample, cache_attn_k, cache_attn_v, state_delta, state_delta_conv, state_hgrn, state_pool,
           norm_w, w_in, w_out, delta_conv_w, delta_a_log, delta_dt_bias, delta_norm_w, hgrn_lb_raw, hgrn_norm_w,
           pool_w, pool_scale, final_norm_w):
    w_in_p, w_out_b, pw_bd = _prep_weights(w_in, w_out, pool_w)
    prm = dict(
        norm_w=norm_w, w_in=w_in_p, w_out=w_out_b, pool_w_bd=pw_bd, pool_scale=pool_scale,
        final_norm_w=final_norm_w, delta_conv_w=delta_conv_w, hgrn_lb_raw=hgrn_lb_raw,
        alog_row=_lane_row(delta_a_log, N_HEADS), dtb_row=_lane_row(delta_dt_bias, N_HEADS),
        delta_nw_row=jnp.tile(delta_norm_w, (1, N_HEADS))[:, None, :],
        hgrn_nw_row=jnp.tile(hgrn_norm_w, (1, N_HEADS))[:, None, :],
    )
    bp = x_prompt.shape[0]
    y_p, st_p = _trunk(x_prompt, 0, None, prm, dict(tm=512, ns=bp, tl=256, ns_post=1, tl_post=512, swa_sub=4))
    dec_b, dec_l = x_sample.shape[0], x_sample.shape[1]
    depth, _, buf = cache_attn_k.shape[0:3]
    ckt = jnp.transpose(cache_attn_k, (0, 1, 3, 4, 2)).reshape(depth, dec_b, BRANCH, buf)
    cvt = jnp.transpose(cache_attn_v, (0, 1, 3, 4, 2)).reshape(depth, dec_b, BRANCH, buf)
    y_s, st_s = _trunk(x_sample, PAST_LEN, (ckt, cvt, state_delta, state_delta_conv, state_hgrn, state_pool), prm,
                       dict(tm=256, ns=32, tl=dec_l, ns_post=32, tl_post=dec_l, attn_nb=4))
    k_p, v_p, d_p, dc_p, h_p, pl_p = st_p
    k_s, v_s, d_s, dc_s, h_s, pl_s = st_s
    return (y_p, y_s, k_p, k_s, v_p, v_s, d_p, d_s, dc_p, dc_s, h_p, h_s, pl_p, pl_s)
```

```python
import functools
import math

import jax
import jax.numpy as jnp
from jax import lax
from jax.experimental import pallas as pl
from jax.experimental.pallas import tpu as pltpu

F32 = jnp.float32
BF16 = jnp.bfloat16

N_HEADS = 4
HEAD_DIM = 64
BRANCH = N_HEADS * HEAD_DIM
ATTN_PATTERNS = ((128, 1), (512, 4), (2048, 16))
ATTN_STEPS = 128
CONV_TAPS = 4
DELTA_CHUNK = 64
HGRN_CHUNK = 16
POOL_WINDOWS = (2, 4, 8, 16)
POOL_MAX = 16
PAST_LEN = 2048
NORM_EPS = 1e-6
NEG_MASK = -1e30
ALIBI_SLOPES = tuple(2.0 ** (-8.0 * (h + 1) / N_HEADS) for h in range(N_HEADS))

SUBLANES = 8
LANES = 128
VMEM_LIMIT = 56 * 1024 * 1024

SEG_A = (0, 4 * BRANCH)
SEG_B = (4 * BRANCH, 8 * BRANCH)
SEG_C = (8 * BRANCH, 12 * BRANCH)
SEG_D = (12 * BRANCH, 14 * BRANCH)
SEG_E = (14 * BRANCH, 14 * BRANCH + LANES)
IN_PAD = SEG_E[1]
DILATED = tuple(d for _, d in ATTN_PATTERNS if d > 1)


def _sigmoid(x):
    return 1.0 / (1.0 + jnp.exp(-x))


def _silu(x):
    return x * _sigmoid(x)


def _softplus(x):
    return jnp.maximum(x, 0.0) + jnp.log(1.0 + jnp.exp(-jnp.abs(x)))


def _head_ones():
    r = lax.broadcasted_iota(jnp.int32, (BRANCH, BRANCH), 0) // HEAD_DIM
    c = lax.broadcasted_iota(jnp.int32, (BRANCH, BRANCH), 1) // HEAD_DIM
    return jnp.where(r == c, 1.0, 0.0).astype(BF16)


def _split2(x):
    hi = x.astype(BF16)
    return hi, (x - hi.astype(F32)).astype(BF16)


def _split3(x):
    t1 = x.astype(BF16)
    r1 = x - t1.astype(F32)
    t2 = r1.astype(BF16)
    return t1, t2, (r1 - t2.astype(F32)).astype(BF16)


def _head_sum(x, g):
    hi, lo = _split2(x)
    return jnp.dot(hi, g, preferred_element_type=F32) + jnp.dot(lo, g, preferred_element_type=F32)


def _bmm(a, b):
    return jnp.einsum('bij,bjk->bik', a, b, preferred_element_type=F32)


def _bmm_nt(a, b):
    return jnp.einsum('bid,bjd->bij', a, b, preferred_element_type=F32)


def _bmm_tn(a, b):
    return jnp.einsum('bci,bcj->bij', a, b, preferred_element_type=F32)


def _bmm1(a, b):
    return _bmm(a.astype(BF16), b.astype(BF16))


def _bmm3(a, b):
    ah, al = _split2(a)
    bh, bl = _split2(b)
    return _bmm(ah, bh) + _bmm(ah, bl) + _bmm(al, bh)


def _to_head_batch(x2d, nb, chunk):
    parts = [x2d[:, h * HEAD_DIM:(h + 1) * HEAD_DIM].reshape(nb, 1, chunk, HEAD_DIM) for h in range(N_HEADS)]
    return jnp.concatenate(parts, axis=1).reshape(nb * N_HEADS, chunk, HEAD_DIM)


def _from_head_batch(x, nb, chunk):
    x4 = x.reshape(nb, N_HEADS, chunk, HEAD_DIM)
    return jnp.concatenate([x4[:, h].reshape(nb * chunk, HEAD_DIM) for h in range(N_HEADS)], axis=-1)


def _inproj_body(*refs, tm, prompt):
    x_ref, nw_ref, w_ref = refs[0:3]
    first_out = 5 if prompt else 3
    pa_ref, pb_ref, pc_ref, pd_ref, pe_ref = refs[first_out:first_out + 5]
    x = x_ref[...]
    ms = jnp.mean(x * x, axis=-1, keepdims=True)
    h = (x * lax.rsqrt(ms + NORM_EPS) * nw_ref[...]).astype(BF16)
    for ref, (lo, hi) in ((pb_ref, SEG_B), (pc_ref, SEG_C), (pd_ref, SEG_D), (pe_ref, SEG_E)):
        ref[...] = jnp.dot(h, w_ref[:, lo:hi], preferred_element_type=F32)
    p_a = jnp.dot(h, w_ref[:, SEG_A[0]:SEG_A[1]], preferred_element_type=F32)
    pa_ref[...] = p_a
    if not prompt:
        return
    class_refs = refs[first_out + 5:first_out + 5 + len(DILATED)]
    k_keep, v_keep, slab = refs[first_out + 5 + len(DILATED):]
    k_keep[...] = p_a[:, BRANCH:2 * BRANCH]
    v_keep[...] = p_a[:, 2 * BRANCH:3 * BRANCH]
    n_slab = 3 * BRANCH // LANES
    for c in range(n_slab):
        slab[c] = p_a[:, c * LANES:(c + 1) * LANES]
    for ref, dil in zip(class_refs, DILATED):
        for r in range(dil):
            for c in range(n_slab):
                ref[0, r, :, c * LANES:(c + 1) * LANES] = slab[c, pl.ds(r, tm // dil, stride=dil), :].astype(BF16)


def _inproj(x2d, nw, w, tm, seq_len=None, keep=None):
    t, d = x2d.shape
    widths = [s[1] - s[0] for s in (SEG_A, SEG_B, SEG_C, SEG_D, SEG_E)]
    out_specs = [pl.BlockSpec((tm, wd), lambda i: (i, 0)) for wd in widths]
    out_shape = [jax.ShapeDtypeStruct((t, wd), F32) for wd in widths]
    in_specs = [pl.BlockSpec((tm, d), lambda i: (i, 0)),
                pl.BlockSpec((1, d), lambda i: (0, 0)),
                pl.BlockSpec((d, IN_PAD), lambda i: (0, 0))]
    args = [x2d, nw, w]
    aliases, scratch = {}, []
    prompt = seq_len is not None
    if prompt:
        tps = seq_len // tm
        for dil in DILATED:
            out_specs.append(pl.BlockSpec((1, dil, tm // dil, 3 * BRANCH), lambda i: (i // tps, 0, i % tps, 0)))
            out_shape.append(jax.ShapeDtypeStruct((t // seq_len, dil, seq_len // dil, 3 * BRANCH), BF16))
        k_stack, v_stack, layer = keep
        skip = tps - k_stack.shape[2] // tm
        for stack in (k_stack, v_stack):
            aliases[len(args)] = len(out_shape)
            args.append(stack)
            in_specs.append(pl.BlockSpec(memory_space=pl.ANY))
            out_specs.append(pl.BlockSpec((None, None, tm, BRANCH),
                                          lambda i: (layer, i // tps, jnp.maximum(i % tps - skip, 0), 0)))
            out_shape.append(jax.ShapeDtypeStruct(stack.shape, stack.dtype))
        scratch = [pltpu.VMEM((3 * BRANCH // LANES, tm, LANES), F32)]
    return pl.pallas_call(
        functools.partial(_inproj_body, tm=tm, prompt=prompt),
        grid=(t // tm,),
        in_specs=in_specs,
        out_specs=out_specs,
        out_shape=out_shape,
        input_output_aliases=aliases,
        scratch_shapes=scratch,
        compiler_params=pltpu.CompilerParams(dimension_semantics=("arbitrary",), vmem_limit_bytes=VMEM_LIMIT),
        name="inproj",
    )(*args)


def _swa_body(q_ref, kp_ref, kc_ref, vp_ref, vc_ref, o_ref, lse_ref, *, dil, tq, n_sub):
    n = pl.program_id(1)
    q = (q_ref[...].astype(F32) * (HEAD_DIM ** -0.5)).astype(BF16)
    k = jnp.concatenate([kp_ref[...], kc_ref[...]], axis=0).astype(BF16)
    v = jnp.concatenate([vp_ref[...], vc_ref[...]], axis=0).astype(BF16)
    iq = lax.broadcasted_iota(jnp.int32, (tq, 2 * tq), 0)
    jk = lax.broadcasted_iota(jnp.int32, (tq, 2 * tq), 1)
    dist = iq + tq - jk
    in_band = (dist >= 0) & (dist <= ATTN_STEPS)
    bias = (dist * dil).astype(F32)
    head_bias = [ALIBI_SLOPES[h] * bias for h in range(N_HEADS)]
    first_valid = in_band & ((jk >= tq) | (n > 0))
    ones = jnp.ones((v.shape[0], HEAD_DIM), BF16)
    v_aug = [jnp.concatenate([v[:, h * HEAD_DIM:(h + 1) * HEAD_DIM], ones], axis=-1) for h in range(N_HEADS)]
    for i in range(n_sub):
        valid = first_valid if i == 0 else in_band
        rows = slice(i * tq, (i + 1) * tq)
        keys = slice(i * tq, (i + 2) * tq)
        outs, lses = [], []
        for h in range(N_HEADS):
            sl = slice(h * HEAD_DIM, (h + 1) * HEAD_DIM)
            s = lax.dot_general(q[rows, sl], k[keys, sl], (((1,), (1,)), ((), ())), preferred_element_type=F32)
            s = jnp.where(valid, s - head_bias[h], NEG_MASK)
            m = jnp.max(s, axis=-1, keepdims=True)
            p = jnp.exp(s - m)
            nd = jnp.dot(p.astype(BF16), v_aug[h][keys], preferred_element_type=F32)
            den = nd[:, HEAD_DIM:2 * HEAD_DIM]
            outs.append(nd[:, 0:HEAD_DIM] / den)
            lses.append(m + jnp.log(den))
        o_ref[rows, :] = jnp.concatenate(outs, axis=-1)
        lse_ref[rows, :] = jnp.concatenate(lses, axis=-1)


def _swa(qkv, dil, n_sub):
    b, _, n, _ = qkv.shape
    tq = ATTN_STEPS
    tqb = tq * n_sub

    def cur(col):
        return pl.BlockSpec((None, None, tqb, BRANCH), lambda zi, ni: (zi // dil, zi % dil, ni, col))

    def prev(col):
        return pl.BlockSpec((None, None, tq, BRANCH),
                            lambda zi, ni: (zi // dil, zi % dil, jnp.maximum(ni * n_sub - 1, 0), col))

    shp = jax.ShapeDtypeStruct((b, dil, n, BRANCH), F32)
    return pl.pallas_call(
        functools.partial(_swa_body, dil=dil, tq=tq, n_sub=n_sub),
        grid=(b * dil, n // tqb),
        in_specs=[cur(0), prev(1), cur(1), prev(2), cur(2)],
        out_specs=[cur(0), cur(0)],
        out_shape=[shp, shp],
        compiler_params=pltpu.CompilerParams(dimension_semantics=("arbitrary", "arbitrary"),
                                             vmem_limit_bytes=VMEM_LIMIT),
        name=f"swa_d{dil}",
    )(qkv, qkv, qkv, qkv, qkv)


def _merge_patterns(o_refs, lse_refs, scratch, tm):
    outs, lses = [], []
    si = 0
    for (_, dil), o_ref, lse_ref in zip(ATTN_PATTERNS, o_refs, lse_refs):
        if dil == 1:
            outs.append(o_ref[...])
            lses.append(lse_ref[...])
            continue
        ob, lb = scratch[si], scratch[si + 1]
        si += 2
        n_slab = BRANCH // LANES
        for r in range(dil):
            for c in range(n_slab):
                ob[c, pl.ds(r, tm // dil, stride=dil), :] = o_ref[r, :, c * LANES:(c + 1) * LANES]
                lb[c, pl.ds(r, tm // dil, stride=dil), :] = lse_ref[r, :, c * LANES:(c + 1) * LANES]
        outs.append(jnp.concatenate([ob[c] for c in range(n_slab)], axis=-1))
        lses.append(jnp.concatenate([lb[c] for c in range(n_slab)], axis=-1))
    l_all = lses[0]
    for l in lses[1:]:
        l_all = jnp.maximum(l_all, l)
    numer = jnp.zeros_like(l_all)
    denom = jnp.zeros_like(l_all)
    for o, l in zip(outs, lses):
        wgt = jnp.exp(l - l_all)
        numer = numer + wgt * o
        denom = denom + wgt
    return numer / denom


def _attn_sample_one(pa, kt, vt, t_new, buf):
    rows = N_HEADS * t_new
    q = pa[:, 0:BRANCH] * (HEAD_DIM ** -0.5)
    k_new = pa[:, BRANCH:2 * BRANCH]
    v_new = pa[:, 2 * BRANCH:3 * BRANCH]
    gate = pa[:, 3 * BRANCH:4 * BRANCH]
    row_h = lax.broadcasted_iota(jnp.int32, (rows, BRANCH), 0) // t_new
    lane_h = lax.broadcasted_iota(jnp.int32, (rows, BRANCH), 1) // HEAD_DIM
    own = row_h == lane_h
    q_bd = jnp.where(own, jnp.concatenate([q] * N_HEADS, axis=0), 0.0).astype(BF16)
    pad = jnp.zeros((LANES - t_new, BRANCH), F32)
    k_new_p = jnp.concatenate([k_new, pad], axis=0).astype(BF16)
    v_new_p = jnp.concatenate([v_new, pad], axis=0).astype(BF16)
    kt = kt.astype(BF16)
    vt = vt.astype(BF16)
    nt = (((1,), (1,)), ((), ()))
    s = jnp.concatenate([jnp.dot(q_bd, kt, preferred_element_type=F32),
                         lax.dot_general(q_bd, k_new_p, nt, preferred_element_type=F32)], axis=-1)
    ncol = buf + LANES
    col = lax.broadcasted_iota(jnp.int32, (rows, ncol), 1)
    row = lax.broadcasted_iota(jnp.int32, (rows, ncol), 0)
    delta = buf + row % t_new - col
    hrow = row // t_new
    slope = jnp.where(hrow == 0, ALIBI_SLOPES[0],
                      jnp.where(hrow == 1, ALIBI_SLOPES[1], jnp.where(hrow == 2, ALIBI_SLOPES[2], ALIBI_SLOPES[3])))
    s = s - slope * delta.astype(F32)
    ps, ms, dens = [], [], []
    for win, dil in ATTN_PATTERNS:
        valid = (delta >= 0) & (delta <= win) & ((delta & (dil - 1)) == 0)
        sp = jnp.where(valid, s, NEG_MASK)
        m = jnp.max(sp, axis=-1, keepdims=True)
        p = jnp.exp(sp - m)
        ps.append(p.astype(BF16))
        ms.append(m)
        dens.append(jnp.sum(p, axis=-1, keepdims=True))
    p_all = jnp.concatenate(ps, axis=0)
    num_all = (lax.dot_general(p_all[:, :buf], vt, nt, preferred_element_type=F32)
               + jnp.dot(p_all[:, buf:], v_new_p, preferred_element_type=F32))
    m_all = jnp.maximum(jnp.maximum(ms[0], ms[1]), ms[2])
    numer = jnp.zeros((rows, BRANCH), F32)
    denom = jnp.zeros((rows, 1), F32)
    for g in range(len(ATTN_PATTERNS)):
        wgt = jnp.exp(ms[g] - m_all)
        numer = numer + wgt * num_all[g * rows:(g + 1) * rows]
        denom = denom + wgt * dens[g]
    o_full = jnp.where(own, numer / denom, 0.0)
    o = o_full[0:t_new]
    for h in range(1, N_HEADS):
        o = o + o_full[h * t_new:(h + 1) * t_new]
    return o * _silu(gate)


def _attn_sample_body(pa_ref, kt_ref, vt_ref, out_ref, *, t_new, buf, nb):
    for j in range(nb):
        out_ref[j] = _attn_sample_one(pa_ref[j], kt_ref[j], vt_ref[j], t_new, buf)


def _delta_body(pb_ref, pe_ref, cpre_ref, s0_ref, cw_ref, alog_ref, dtb_ref, nw_ref, _stack_ref, y_ref, s_out_ref,
                xbuf, s_scr, *, ns, tl, chunk, n_tiles):
    t = pl.program_id(1)
    nc = tl // chunk
    nb = ns * nc
    nbh = nb * N_HEADS
    rows = ns * tl
    width = 3 * BRANCH

    @pl.when(t == 0)
    def _():
        xbuf[:, 0:SUBLANES, :] = cpre_ref[...]
        s_scr[...] = s0_ref[...]

    xbuf[:, SUBLANES:SUBLANES + tl, :] = pb_ref[:, :, 0:width]
    conv = jnp.zeros((ns, tl, width), F32)
    for tap in range(CONV_TAPS):
        off = SUBLANES - (CONV_TAPS - 1) + tap
        conv = conv + xbuf[:, off:off + tl, :] * cw_ref[tap:tap + 1, :]
    if n_tiles > 1:
        xbuf[:, 0:SUBLANES, :] = xbuf[:, tl:tl + SUBLANES, :]
    conv = _silu(conv).reshape(rows, width)
    g_ones = _head_ones()
    q = conv[:, 0:BRANCH]
    k = conv[:, BRANCH:2 * BRANCH]
    v = conv[:, 2 * BRANCH:3 * BRANCH]
    q = q * lax.rsqrt(_head_sum(q * q, g_ones) + 1e-6) * (HEAD_DIM ** -0.5)
    k = k * lax.rsqrt(_head_sum(k * k, g_ones) + 1e-6)
    pe = pe_ref[...].reshape(rows, LANES)
    lane = lax.broadcasted_iota(jnp.int32, (rows, LANES), 1)
    gb = jnp.where(lane < N_HEADS, _sigmoid(pe), -jnp.exp(alog_ref[...]) * _softplus(pe + dtb_ref[...]))

    qh = _to_head_batch(q, nb, chunk)
    kh = _to_head_batch(k, nb, chunk)
    vh = _to_head_batch(v, nb, chunk)

    def col_batch(first):
        cols = [gb[:, first + h:first + h + 1].reshape(nb, 1, chunk, 1) for h in range(N_HEADS)]
        return jnp.concatenate(cols, axis=1).reshape(nbh, chunk, 1)

    beta = col_batch(0)
    g = col_batch(N_HEADS)
    ri = lax.broadcasted_iota(jnp.int32, (nbh, chunk, chunk), 1)
    ci = lax.broadcasted_iota(jnp.int32, (nbh, chunk, chunk), 2)
    ltri = jnp.where(ri >= ci, 1.0, 0.0).astype(BF16)
    eye = jnp.where(ri == ci, 1.0, 0.0).astype(F32)
    gm = jnp.where(ri > ci, jnp.broadcast_to(g, (nbh, chunk, chunk)), 0.0)
    dmat = sum(_bmm(ltri, term) for term in _split3(gm))
    gc = dmat[:, :, 0:1] + g[:, 0:1, :]
    g_last = gc[:, chunk - 1:chunk, :]
    decay = jnp.where(ri >= ci, jnp.exp(dmat), 0.0)
    kb = kh * beta
    vb = vh * beta
    kh_b = kh.astype(BF16)
    a_mat = jnp.where(ri > ci, _bmm_nt(kb.astype(BF16), kh_b) * decay, 0.0)
    n_fac = int(math.log2(chunk))
    x = -a_mat
    u = eye + x
    x = _bmm1(x, x)
    for _ in range(2, n_fac):
        prod = _bmm1(jnp.concatenate([x, u], axis=1), x)
        x = prod[:, 0:chunk]
        u = u + prod[:, chunk:2 * chunk]
    t_inv = u + _bmm1(u, x)
    uw = _bmm3(t_inv, jnp.concatenate([vb, kb * jnp.exp(gc)], axis=-1))
    u_mat = uw[:, :, 0:HEAD_DIM]
    w_mat = uw[:, :, HEAD_DIM:2 * HEAD_DIM]
    qk = _bmm_nt(qh.astype(BF16), kh_b) * decay
    qg = qh * jnp.exp(gc)
    kg = kh * jnp.exp(g_last - gc)
    e_last = jnp.exp(g_last)

    def sel(a, c):
        a5 = a.reshape((ns, nc, N_HEADS) + a.shape[1:])
        return a5[:, c].reshape((ns * N_HEADS,) + a.shape[1:])

    st = s_scr[...].reshape(ns * N_HEADS, HEAD_DIM, HEAD_DIM)
    o_chunks = []
    for c in range(nc):
        s_b = st.astype(BF16)
        wq = _bmm(jnp.concatenate([sel(w_mat, c), sel(qg, c)], axis=1).astype(BF16), s_b)
        v_new = (sel(u_mat, c) - wq[:, 0:chunk]).astype(BF16)
        o_chunks.append(wq[:, chunk:2 * chunk] + _bmm(sel(qk, c).astype(BF16), v_new))
        st = st * sel(e_last, c) + _bmm_tn(sel(kg, c).astype(BF16), v_new)
    s_scr[...] = st.reshape(ns, N_HEADS, HEAD_DIM, HEAD_DIM)
    if nc > 1:
        o_all = jnp.concatenate([oc.reshape(ns, 1, N_HEADS, chunk, HEAD_DIM) for oc in o_chunks], axis=1)
        o_all = o_all.reshape(nbh, chunk, HEAD_DIM)
    else:
        o_all = o_chunks[0]
    o = _from_head_batch(o_all, nb, chunk)
    o = o * lax.rsqrt(_head_sum(o * o, g_ones) * (1.0 / HEAD_DIM) + NORM_EPS) * nw_ref[...]
    gate = pb_ref[:, :, width:width + BRANCH].reshape(rows, BRANCH)
    y_ref[...] = (o * _silu(gate)).reshape(ns, tl, BRANCH)

    @pl.when(t == n_tiles - 1)
    def _():
        s_out_ref[...] = s_scr[...]


def _state_spec(ns, layer):
    return pl.BlockSpec((None, ns, N_HEADS, HEAD_DIM, HEAD_DIM), lambda bi, ti: (layer, bi, 0, 0, 0))


def _delta(pb, pe, cpre, s0, s0_layer, s_stack, layer, cw, alog_row, dtb_row, nw_row, *, ns, tl, chunk):
    b, l, _ = pb.shape
    n_tiles = l // tl
    seq3 = lambda w: pl.BlockSpec((ns, tl, w), lambda bi, ti: (bi, ti, 0))
    const2 = lambda a: pl.BlockSpec(a.shape, lambda bi, ti: (0, 0))
    return pl.pallas_call(
        functools.partial(_delta_body, ns=ns, tl=tl, chunk=chunk, n_tiles=n_tiles),
        grid=(b // ns, n_tiles),
        in_specs=[seq3(4 * BRANCH), seq3(LANES),
                  pl.BlockSpec((ns, SUBLANES, 3 * BRANCH), lambda bi, ti: (bi, 0, 0)), _state_spec(ns, s0_layer),
                  const2(cw), const2(alog_row), const2(dtb_row), const2(nw_row),
                  pl.BlockSpec(memory_space=pl.ANY)],
        out_specs=[seq3(BRANCH), _state_spec(ns, layer)],
        out_shape=[jax.ShapeDtypeStruct((b, l, BRANCH), F32),
                   jax.ShapeDtypeStruct(s_stack.shape, F32)],
        input_output_aliases={8: 1},
        scratch_shapes=[pltpu.VMEM((ns, SUBLANES + tl, 3 * BRANCH), F32),
                        pltpu.VMEM((ns, N_HEADS, HEAD_DIM, HEAD_DIM), F32)],
        compiler_params=pltpu.CompilerParams(dimension_semantics=("arbitrary", "arbitrary"),
                                             vmem_limit_bytes=VMEM_LIMIT),
        name="delta",
    )(pb, pe, cpre, s0, cw, alog_row, dtb_row, nw_row, s_stack)


def _hgrn_body(pc_ref, lbraw_ref, s0_ref, nw_ref, _stack_ref, y_ref, s_out_ref, s_scr,
               *, ns, tl, chunk, n_tiles, layer):
    t = pl.program_id(1)
    nc = tl // chunk
    nb = ns * nc
    rows = ns * tl

    def flip(s4):
        return jnp.swapaxes(s4.reshape(ns * N_HEADS, HEAD_DIM, HEAD_DIM), 1, 2).reshape(s4.shape)

    @pl.when(t == 0)
    def _():
        s_scr[...] = flip(s0_ref[...])

    raw = lbraw_ref[...]
    e = jnp.exp(raw - jnp.max(raw, axis=0, keepdims=True))
    sm = e / jnp.sum(e, axis=0, keepdims=True)
    lb = jnp.zeros((1, BRANCH), F32)
    for d in range(1, layer + 1):
        lb = lb + sm[d:d + 1, :]

    pc = pc_ref[...].reshape(rows, 4 * BRANCH)
    qh = _silu(pc[:, 0:BRANCH])
    fr = pc[:, BRANCH:2 * BRANCH]
    vh = pc[:, 2 * BRANCH:3 * BRANCH]
    gate = pc[:, 3 * BRANCH:4 * BRANCH]
    f = lb + (1.0 - lb) * _sigmoid(fr)
    log_f = jnp.log(f).reshape(nb, chunk, BRANCH)
    kh = (1.0 - lb) * _sigmoid(-fr)

    q3 = qh.reshape(nb, chunk, BRANCH)
    k3 = kh.reshape(nb, chunk, BRANCH)
    v3 = vh.reshape(nb, chunk, BRANCH)
    ri = lax.broadcasted_iota(jnp.int32, (nb, chunk, chunk), 1)
    ci = lax.broadcasted_iota(jnp.int32, (nb, chunk, chunk), 2)
    ltri = jnp.where(ri >= ci, 1.0, 0.0).astype(BF16)
    l1 = log_f.astype(BF16)
    r1 = log_f - l1.astype(F32)
    l2 = r1.astype(BF16)
    l3 = (r1 - l2.astype(F32)).astype(BF16)
    gcum = _bmm(ltri, l1) + _bmm(ltri, l2) + _bmm(ltri, l3)
    g_last = gcum[:, chunk - 1:chunk, :]
    qg = q3 * jnp.exp(gcum)
    kg = k3 * jnp.exp(g_last - gcum)
    e_last = jnp.exp(g_last)

    g_ones = _head_ones()
    n_grp = chunk // SUBLANES
    row_i = [g * SUBLANES + lax.broadcasted_iota(jnp.int32, (nb, chunk - g * SUBLANES, BRANCH), 1)
             for g in range(n_grp)]
    acc = [jnp.zeros((nb, SUBLANES, BRANCH), F32) for _ in range(n_grp)]
    for j in range(chunk):
        g0 = j // SUBLANES
        lo = g0 * SUBLANES
        dec = jnp.exp(jnp.where(row_i[g0] >= j, gcum[:, lo:, :] - gcum[:, j:j + 1, :], NEG_MASK))
        t_j = (q3[:, lo:, :] * dec * k3[:, j:j + 1, :]).reshape(nb * (chunk - lo), BRANCH)
        a_j = jnp.dot(t_j.astype(BF16), g_ones, preferred_element_type=F32).reshape(nb, chunk - lo, BRANCH)
        c_j = a_j * v3[:, j:j + 1, :]
        for gi in range(g0, n_grp):
            acc[gi] = acc[gi] + c_j[:, (gi - g0) * SUBLANES:(gi - g0 + 1) * SUBLANES, :]
    intra = jnp.concatenate(acc, axis=1) if n_grp > 1 else acc[0]

    qg_b = _to_head_batch(qg.reshape(rows, BRANCH), nb, chunk).astype(BF16)
    kg_b = _to_head_batch(kg.reshape(rows, BRANCH), nb, chunk).astype(BF16)
    v_b = _to_head_batch(vh, nb, chunk).astype(BF16)
    el_b = _to_head_batch(e_last.reshape(nb, BRANCH), nb, 1)

    def sel(a, c):
        a5 = a.reshape((ns, nc, N_HEADS) + a.shape[1:])
        return a5[:, c].reshape((ns * N_HEADS,) + a.shape[1:])

    st = s_scr[...].reshape(ns * N_HEADS, HEAD_DIM, HEAD_DIM)
    o_chunks = []
    for c in range(nc):
        o_chunks.append(_bmm_nt(sel(qg_b, c), st.astype(BF16)))
        st = st * sel(el_b, c) + _bmm_tn(sel(v_b, c), sel(kg_b, c))
    s_scr[...] = st.reshape(ns, N_HEADS, HEAD_DIM, HEAD_DIM)
    if nc > 1:
        o_all = jnp.concatenate([oc.reshape(ns, 1, N_HEADS, chunk, HEAD_DIM) for oc in o_chunks], axis=1)
        o_all = o_all.reshape(nb * N_HEADS, chunk, HEAD_DIM)
    else:
        o_all = o_chunks[0]
    o = _from_head_batch(o_all, nb, chunk) + intra.reshape(rows, BRANCH)
    o = o * lax.rsqrt(_head_sum(o * o, g_ones) * (1.0 / HEAD_DIM) + NORM_EPS) * nw_ref[...]
    y_ref[...] = (o * _silu(gate)).reshape(ns, tl, BRANCH)

    @pl.when(t == n_tiles - 1)
    def _():
        s_out_ref[...] = flip(s_scr[...])


def _hgrn_attn_body(pc_ref, lbraw_ref, s0_ref, nw_ref, stack_ref, pa_ref, kt_ref, vt_ref, y_ref, s_out_ref, ya_ref,
                    s_scr, *, hgrn_args, attn_args):
    _hgrn_body(pc_ref, lbraw_ref, s0_ref, nw_ref, stack_ref, y_ref, s_out_ref, s_scr, **hgrn_args)
    _attn_sample_body(pa_ref, kt_ref, vt_ref, ya_ref, **attn_args)


def _hgrn(pc, lb_raw, s0, s0_layer, s_stack, nw_row, *, ns, tl, chunk, layer, rider=None):
    b, l, _ = pc.shape
    n_tiles = l // tl
    hgrn_args = dict(ns=ns, tl=tl, chunk=chunk, n_tiles=n_tiles, layer=layer)
    in_specs = [pl.BlockSpec((ns, tl, 4 * BRANCH), lambda bi, ti: (bi, ti, 0)),
                pl.BlockSpec(lb_raw.shape, lambda bi, ti: (0, 0)), _state_spec(ns, s0_layer),
                pl.BlockSpec(nw_row.shape, lambda bi, ti: (0, 0)),
                pl.BlockSpec(memory_space=pl.ANY)]
    out_specs = [pl.BlockSpec((ns, tl, BRANCH), lambda bi, ti: (bi, ti, 0)), _state_spec(ns, layer)]
    out_shape = [jax.ShapeDtypeStruct((b, l, BRANCH), F32), jax.ShapeDtypeStruct(s_stack.shape, F32)]
    args = [pc, lb_raw, s0, nw_row, s_stack]
    body = functools.partial(_hgrn_body, **hgrn_args)
    if rider is not None:
        pa_s, cache_kt, cache_vt = rider
        assert b == ns and pa_s.shape[0] % n_tiles == 0
        nb = pa_s.shape[0] // n_tiles
        t_new, buf = pa_s.shape[1], cache_kt.shape[3]
        cache = pl.BlockSpec((None, nb, BRANCH, buf), lambda bi, ti: (layer, ti, 0, 0))
        in_specs += [pl.BlockSpec((nb, t_new, 4 * BRANCH), lambda bi, ti: (ti, 0, 0)), cache, cache]
        out_specs.append(pl.BlockSpec((nb, t_new, BRANCH), lambda bi, ti: (ti, 0, 0)))
        out_shape.append(jax.ShapeDtypeStruct((pa_s.shape[0], t_new, BRANCH), F32))
        args += [pa_s, cache_kt, cache_vt]
        body = functools.partial(_hgrn_attn_body, hgrn_args=hgrn_args, attn_args=dict(t_new=t_new, buf=buf, nb=nb))
    return pl.pallas_call(
        body,
        grid=(b // ns, n_tiles),
        in_specs=in_specs,
        out_specs=out_specs,
        out_shape=out_shape,
        input_output_aliases={4: 1},
        scratch_shapes=[pltpu.VMEM((ns, N_HEADS, HEAD_DIM, HEAD_DIM), F32)],
        compiler_params=pltpu.CompilerParams(dimension_semantics=("arbitrary", "arbitrary"),
                                             vmem_limit_bytes=VMEM_LIMIT),
        name="hgrn" if rider is None else "hgrn_attn",
    )(*args)


def _post_body(*refs, ns, tl, n_tiles, start_pos, final, n_attn):
    attn_refs = refs[0:n_attn]
    (x_ref, yb_ref, yc_ref, pd_ref, ppre_ref, pw_ref, ps_ref, wo_ref, fw_ref, out_ref, xbuf) = refs[n_attn:n_attn + 11]
    slabs = refs[n_attn + 11:]
    t = pl.program_id(1)
    rows = ns * tl
    if n_attn == 1:
        ya = attn_refs[0][...].reshape(rows, BRANCH)
    else:
        n_g = len(ATTN_PATTERNS)
        ya = _merge_patterns(attn_refs[0:n_g], attn_refs[n_g:2 * n_g], slabs, tl) * _silu(attn_refs[2 * n_g][...])

    @pl.when(t == 0)
    def _():
        xbuf[:, 0:POOL_MAX, :] = ppre_ref[...]

    xbuf[:, POOL_MAX:POOL_MAX + tl, :] = pd_ref[:, :, 0:BRANCH]

    def back(kk):
        return xbuf[:, POOL_MAX - kk:POOL_MAX - kk + tl, :]

    x0 = back(0)
    sums = {}
    acc = x0
    for kk in range(1, POOL_MAX):
        acc = acc + back(kk)
        if kk + 1 in POOL_WINDOWS:
            sums[kk + 1] = acc
    if n_tiles > 1:
        xbuf[:, 0:POOL_MAX, :] = xbuf[:, tl:tl + POOL_MAX, :]
    group = lax.broadcasted_iota(jnp.int32, (ns, tl, BRANCH), 2) // HEAD_DIM
    pos = start_pos + t * tl + lax.broadcasted_iota(jnp.int32, (ns, tl, BRANCH), 1)
    tot = sums[POOL_WINDOWS[-1]]
    win = jnp.full((ns, tl, BRANCH), POOL_WINDOWS[-1], jnp.int32)
    for gi in range(len(POOL_WINDOWS) - 2, -1, -1):
        tot = jnp.where(group == gi, sums[POOL_WINDOWS[gi]], tot)
        win = jnp.where(group == gi, POOL_WINDOWS[gi], win)
    cnt = jnp.minimum(pos + 1, win).astype(F32)
    pooled = (tot / cnt - x0).reshape(rows, BRANCH)
    gate_d = pd_ref[:, :, BRANCH:2 * BRANCH].reshape(rows, BRANCH)
    yd = jnp.dot(pooled.astype(BF16), pw_ref[...], preferred_element_type=F32) * ps_ref[...] * _silu(gate_d)
    ycat = jnp.concatenate([ya, yb_ref[...].reshape(rows, BRANCH),
                            yc_ref[...].reshape(rows, BRANCH), yd], axis=-1).astype(BF16)
    x_new = x_ref[...].reshape(rows, -1) + jnp.dot(ycat, wo_ref[...], preferred_element_type=F32)
    if final:
        ms = jnp.mean(x_new * x_new, axis=-1, keepdims=True)
        x_new = x_new * lax.rsqrt(ms + NORM_EPS) * fw_ref[...]
    out_ref[...] = x_new.reshape(out_ref.shape)


def _post(attn, x, yb, yc, pd, ppre, pw_bd, ps_row, wo, fw_row, *, ns, tl, start_pos, final):
    b, l, d = x.shape
    n_tiles = l // tl
    seq3 = lambda w: pl.BlockSpec((ns, tl, w), lambda bi, ti: (bi, ti, 0))
    const2 = lambda a: pl.BlockSpec(a.shape, lambda bi, ti: (0, 0))
    if len(attn) == 1:
        attn_specs, slabs = [seq3(BRANCH)], []
    else:
        assert ns == 1

        def cls(dil):
            if dil == 1:
                return pl.BlockSpec((None, None, tl, BRANCH), lambda bi, ti: (bi, 0, ti, 0))
            return pl.BlockSpec((None, dil, tl // dil, BRANCH), lambda bi, ti: (bi, 0, ti, 0))

        attn_specs = 2 * [cls(dil) for _, dil in ATTN_PATTERNS] + [
            pl.BlockSpec((None, tl, BRANCH), lambda bi, ti: (bi, ti, 3))]
        slabs = [pltpu.VMEM((BRANCH // LANES, tl, LANES), F32)] * (2 * len(DILATED))
    return pl.pallas_call(
        functools.partial(_post_body, ns=ns, tl=tl, n_tiles=n_tiles, start_pos=start_pos, final=final,
                          n_attn=len(attn)),
        grid=(b // ns, n_tiles),
        in_specs=attn_specs + [seq3(d), seq3(BRANCH), seq3(BRANCH), seq3(2 * BRANCH),
                               pl.BlockSpec((ns, POOL_MAX, BRANCH), lambda bi, ti: (bi, 0, 0)),
                               const2(pw_bd), const2(ps_row), const2(wo), const2(fw_row)],
        out_specs=seq3(d),
        out_shape=jax.ShapeDtypeStruct((b, l, d), F32),
        scratch_shapes=[pltpu.VMEM((ns, POOL_MAX + tl, BRANCH), F32)] + slabs,
        compiler_params=pltpu.CompilerParams(dimension_semantics=("arbitrary", "arbitrary"),
                                             vmem_limit_bytes=VMEM_LIMIT),
        name="post",
    )(*attn, x, yb, yc, pd, ppre, pw_bd, ps_row, wo, fw_row)


def _prep_weights(w_in, w_out, pool_w):
    depth, d, _ = w_in.shape
    n_small = 2 * N_HEADS
    main = 8 * BRANCH
    w_perm = jnp.concatenate([
        w_in[:, :, 0:main], w_in[:, :, main + n_small:],
        w_in[:, :, main:main + n_small], jnp.zeros((depth, d, LANES - n_small), w_in.dtype)], axis=-1)
    groups = pool_w.shape[1]
    pw_bd = jnp.zeros((depth, BRANCH, BRANCH), pool_w.dtype)
    for g in range(groups):
        sl = slice(g * HEAD_DIM, (g + 1) * HEAD_DIM)
        pw_bd = pw_bd.at[:, sl, sl].set(pool_w[:, g])
    return w_perm.astype(BF16), w_out.astype(BF16), pw_bd.astype(BF16)


def _attn_prompt(pa3, classes, max_sub):
    b, l, w = pa3.shape
    os_, lses = [], []
    ci = 0
    for _, dil in ATTN_PATTERNS:
        n_sub = min(max_sub, l // dil // ATTN_STEPS)
        if dil == 1:
            o, lse = _swa(pa3.reshape(b, 1, l, w), 1, n_sub)
        else:
            o, lse = _swa(classes[ci], dil, n_sub)
            ci += 1
        os_.append(o)
        lses.append(lse)
    return os_ + lses + [pa3]


def _trunk(x, start_pos, states, prm, cfg):
    b, l, d = x.shape
    depth = prm['norm_w'].shape[0]
    dt = x.dtype
    ns, tl = cfg['ns'], cfg['tl']
    prompt = states is None
    state_shape = (depth, b, N_HEADS, HEAD_DIM, HEAD_DIM)
    d_stack = jnp.zeros(state_shape, F32)
    h_stack = jnp.zeros(state_shape, F32)
    if prompt:
        keep = min(ATTN_PATTERNS[-1][0], l)
        k_stack = jnp.zeros((depth, b, keep, BRANCH), F32)
        v_stack = jnp.zeros((depth, b, keep, BRANCH), F32)
        d_in = h_in = jnp.zeros((1,) + state_shape[1:], F32)
    else:
        c_kt, c_vt, d_in, c_conv, h_in, c_pool = states
    ks, vs, dcs, ps = [], [], [], []
    for layer in range(depth):
        if prompt:
            conv_prefix = jnp.zeros((b, CONV_TAPS - 1, 3 * BRANCH), dt)
            pool_prefix = jnp.zeros((b, POOL_MAX - 1, BRANCH), dt)
        else:
            conv_prefix, pool_prefix = c_conv[layer], c_pool[layer]
        outs = _inproj(x.reshape(b * l, d), prm['norm_w'][layer][None, :], prm['w_in'][layer], cfg['tm'],
                       seq_len=l if prompt else None, keep=(k_stack, v_stack, layer) if prompt else None)
        pa, pb, pc, pd, pe = outs[0:5]
        pa3 = pa.reshape(b, l, 4 * BRANCH)
        s0_layer = 0 if prompt else layer
        hgrn_call = functools.partial(
            _hgrn, pc.reshape(b, l, 4 * BRANCH), prm['hgrn_lb_raw'], h_in, s0_layer, h_stack,
            prm['hgrn_nw_row'][layer], ns=ns, tl=cfg['tl_hgrn'], chunk=math.gcd(l, HGRN_CHUNK), layer=layer)
        if prompt:
            n_cls = len(DILATED)
            attn = _attn_prompt(pa3, outs[5:5 + n_cls], cfg['swa_sub'])
            k_stack, v_stack = outs[5 + n_cls:]
            yc, h_stack = yield hgrn_call
        else:
            attn = [(yield pa3)]
            ks.append(pa3[:, :, BRANCH:2 * BRANCH].reshape(b, l, N_HEADS, HEAD_DIM))
            vs.append(pa3[:, :, 2 * BRANCH:3 * BRANCH].reshape(b, l, N_HEADS, HEAD_DIM))
            yc, h_stack = hgrn_call()

        pb3 = pb.reshape(b, l, 4 * BRANCH)
        cpre = jnp.pad(conv_prefix, ((0, 0), (SUBLANES - (CONV_TAPS - 1), 0), (0, 0)))
        yb, d_stack = _delta(pb3, pe.reshape(b, l, LANES), cpre, d_in, s0_layer, d_stack, layer,
                             prm['delta_conv_w'][layer], prm['alog_row'][layer], prm['dtb_row'][layer],
                             prm['delta_nw_row'][layer], ns=ns, tl=tl, chunk=math.gcd(l, DELTA_CHUNK))
        dcs.append(jnp.concatenate([conv_prefix, pb3[:, :, 0:3 * BRANCH]], axis=1)[:, -(CONV_TAPS - 1):])

        pd3 = pd.reshape(b, l, 2 * BRANCH)
        ppre = jnp.pad(pool_prefix, ((0, 0), (1, 0), (0, 0)))
        ps.append(jnp.concatenate([pool_prefix, pd3[:, :, 0:BRANCH]], axis=1)[:, -(POOL_MAX - 1):])
        x = _post(attn, x, yb, yc, pd3, ppre, prm['pool_w_bd'][layer], prm['pool_scale'][layer][None, :],
                  prm['w_out'][layer], prm['final_norm_w'][None, :], ns=cfg['ns_post'], tl=cfg['tl_post'],
                  start_pos=start_pos, final=(layer == depth - 1))
    if prompt:
        k_all = k_stack.reshape(depth, b, keep, N_HEADS, HEAD_DIM)
        v_all = v_stack.reshape(depth, b, keep, N_HEADS, HEAD_DIM)
    else:
        k_all, v_all = jnp.stack(ks), jnp.stack(vs)
    return x, (k_all, v_all, d_stack, jnp.stack(dcs), h_stack, jnp.stack(ps))


def _lane_row(vals, offset):
    depth, n = vals.shape
    return jnp.zeros((depth, 1, LANES), F32).at[:, 0, offset:offset + n].set(vals.astype(F32))


def kernel(x_prompt, x_sample, cache_attn_k, cache_attn_v, state_delta, state_delta_conv, state_hgrn, state_pool,
           norm_w, w_in, w_out, delta_conv_w, delta_a_log, delta_dt_bias, delta_norm_w, hgrn_lb_raw, hgrn_norm_w,
           pool_w, pool_scale, final_norm_w):
    w_in_p, w_out_b, pw_bd = _prep_weights(w_in, w_out, pool_w)
    prm = dict(
        norm_w=norm_w, w_in=w_in_p, w_out=w_out_b, pool_w_bd=pw_bd, pool_scale=pool_scale,
        final_norm_w=final_norm_w, delta_conv_w=delta_conv_w, hgrn_lb_raw=hgrn_lb_raw,
        alog_row=_lane_row(delta_a_log, N_HEADS), dtb_row=_lane_row(delta_dt_bias, N_HEADS),
        delta_nw_row=jnp.tile(delta_norm_w, (1, N_HEADS))[:, None, :],
        hgrn_nw_row=jnp.tile(hgrn_norm_w, (1, N_HEADS))[:, None, :],
    )
    bp = x_prompt.shape[0]
    dec_b, dec_l = x_sample.shape[0], x_sample.shape[1]
    depth, _, buf = cache_attn_k.shape[0:3]
    ckt = jnp.transpose(cache_attn_k, (0, 1, 3, 4, 2)).reshape(depth, dec_b, BRANCH, buf)
    cvt = jnp.transpose(cache_attn_v, (0, 1, 3, 4, 2)).reshape(depth, dec_b, BRANCH, buf)
    run_p = _trunk(x_prompt, 0, None, prm,
                   dict(tm=512, ns=bp, tl=256, tl_hgrn=128, ns_post=1, tl_post=512, swa_sub=4))
    run_s = _trunk(x_sample, PAST_LEN, (ckt, cvt, state_delta, state_delta_conv, state_hgrn, state_pool), prm,
                   dict(tm=256, ns=32, tl=dec_l, tl_hgrn=dec_l, ns_post=32, tl_post=dec_l))
    hgrn_call, pa_s = next(run_p), next(run_s)
    for _ in range(depth):
        yc, h_stack, ya_s = hgrn_call(rider=(pa_s, ckt, cvt))
        try:
            hgrn_call = run_p.send((yc, h_stack))
        except StopIteration as done:
            y_p, st_p = done.value
        try:
            pa_s = run_s.send(ya_s)
        except StopIteration as done:
            y_s, st_s = done.value
    k_p, v_p, d_p, dc_p, h_p, pl_p = st_p
    k_s, v_s, d_s, dc_s, h_s, pl_s = st_s
    return (y_p, y_s, k_p, k_s, v_p, v_s, d_p, d_s, dc_p, dc_s, h_p, h_s, pl_p, pl_s)
```

```python
import functools
import math

import jax
import jax.numpy as jnp
from jax import lax
from jax.experimental import pallas as pl
from jax.experimental.pallas import tpu as pltpu

F32 = jnp.float32
BF16 = jnp.bfloat16

N_HEADS = 4
HEAD_DIM = 64
BRANCH = N_HEADS * HEAD_DIM
ATTN_PATTERNS = ((128, 1), (512, 4), (2048, 16))
ATTN_STEPS = 128
CONV_TAPS = 4
DELTA_CHUNK = 64
HGRN_CHUNK = 16
POOL_WINDOWS = (2, 4, 8, 16)
POOL_MAX = 16
PAST_LEN = 2048
NORM_EPS = 1e-6
NEG_MASK = -1e30
ALIBI_SLOPES = tuple(2.0 ** (-8.0 * (h + 1) / N_HEADS) for h in range(N_HEADS))

SUBLANES = 8
LANES = 128
VMEM_LIMIT = 56 * 1024 * 1024

P_WIDTHS = (4 * BRANCH, 4 * BRANCH, 4 * BRANCH, 2 * BRANCH, LANES)
DILATED = tuple(d for _, d in ATTN_PATTERNS if d > 1)


def _sigmoid(x):
    return 1.0 / (1.0 + jnp.exp(-x))


def _silu(x):
    return x * _sigmoid(x)


def _softplus(x):
    return jnp.maximum(x, 0.0) + jnp.log(1.0 + jnp.exp(-jnp.abs(x)))


def _head_ones():
    r = lax.broadcasted_iota(jnp.int32, (BRANCH, BRANCH), 0) // HEAD_DIM
    c = lax.broadcasted_iota(jnp.int32, (BRANCH, BRANCH), 1) // HEAD_DIM
    return jnp.where(r == c, 1.0, 0.0).astype(BF16)


def _split2(x):
    hi = x.astype(BF16)
    return hi, (x - hi.astype(F32)).astype(BF16)


def _split3(x):
    t1 = x.astype(BF16)
    r1 = x - t1.astype(F32)
    t2 = r1.astype(BF16)
    return t1, t2, (r1 - t2.astype(F32)).astype(BF16)


def _head_sum(x, g):
    hi, lo = _split2(x)
    return jnp.dot(hi, g, preferred_element_type=F32) + jnp.dot(lo, g, preferred_element_type=F32)


def _bmm(a, b):
    return jnp.einsum('bij,bjk->bik', a, b, preferred_element_type=F32)


def _bmm_nt(a, b):
    return jnp.einsum('bid,bjd->bij', a, b, preferred_element_type=F32)


def _bmm_tn(a, b):
    return jnp.einsum('bci,bcj->bij', a, b, preferred_element_type=F32)


def _bmm1(a, b):
    return _bmm(a.astype(BF16), b.astype(BF16))


def _bmm3(a, b):
    ah, al = _split2(a)
    bh, bl = _split2(b)
    return _bmm(ah, bh) + _bmm(ah, bl) + _bmm(al, bh)


def _to_head_batch(x2d, nb, chunk):
    parts = [x2d[:, h * HEAD_DIM:(h + 1) * HEAD_DIM].reshape(nb, 1, chunk, HEAD_DIM) for h in range(N_HEADS)]
    return jnp.concatenate(parts, axis=1).reshape(nb * N_HEADS, chunk, HEAD_DIM)


def _from_head_batch(x, nb, chunk):
    x4 = x.reshape(nb, N_HEADS, chunk, HEAD_DIM)
    return jnp.concatenate([x4[:, h].reshape(nb * chunk, HEAD_DIM) for h in range(N_HEADS)], axis=-1)


def _inproj_body(*refs, tm, prompt):
    x_ref, nw_ref, wab_ref, wcd_ref, we_ref = refs[0:5]
    first_out = 7 if prompt else 5
    pa_ref, pb_ref, pc_ref, pd_ref, pe_ref = refs[first_out:first_out + 5]
    x = x_ref[...]
    ms = jnp.mean(x * x, axis=-1, keepdims=True)
    h = (x * lax.rsqrt(ms + NORM_EPS) * nw_ref[...]).astype(BF16)
    wide = 4 * BRANCH
    pb_ref[...] = jnp.dot(h, wab_ref[:, wide:2 * wide], preferred_element_type=F32)
    pc_ref[...] = jnp.dot(h, wcd_ref[:, 0:wide], preferred_element_type=F32)
    pd_ref[...] = jnp.dot(h, wcd_ref[:, wide:wide + 2 * BRANCH], preferred_element_type=F32)
    pe_ref[...] = jnp.dot(h, we_ref[...], preferred_element_type=F32)
    p_a = jnp.dot(h, wab_ref[:, 0:wide], preferred_element_type=F32)
    pa_ref[...] = p_a
    if not prompt:
        return
    class_refs = refs[first_out + 5:first_out + 5 + len(DILATED)]
    k_keep, v_keep, slab = refs[first_out + 5 + len(DILATED):]
    k_keep[...] = p_a[:, BRANCH:2 * BRANCH]
    v_keep[...] = p_a[:, 2 * BRANCH:3 * BRANCH]
    n_slab = 3 * BRANCH // LANES
    for c in range(n_slab):
        slab[c] = p_a[:, c * LANES:(c + 1) * LANES]
    for ref, dil in zip(class_refs, DILATED):
        for r in range(dil):
            for c in range(n_slab):
                ref[0, r, :, c * LANES:(c + 1) * LANES] = slab[c, pl.ds(r, tm // dil, stride=dil), :].astype(BF16)


def _inproj(x2d, nw, w, tm, seq_len=None, keep=None):
    t, d = x2d.shape
    widths = P_WIDTHS
    out_specs = [pl.BlockSpec((tm, wd), lambda i: (i, 0)) for wd in widths]
    out_shape = [jax.ShapeDtypeStruct((t, wd), F32) for wd in widths]
    in_specs = [pl.BlockSpec((tm, d), lambda i: (i, 0)),
                pl.BlockSpec((1, d), lambda i: (0, 0))] + [pl.BlockSpec(wi.shape, lambda i: (0, 0)) for wi in w]
    args = [x2d, nw, *w]
    aliases, scratch = {}, []
    prompt = seq_len is not None
    if prompt:
        tps = seq_len // tm
        for dil in DILATED:
            out_specs.append(pl.BlockSpec((1, dil, tm // dil, 3 * BRANCH), lambda i: (i // tps, 0, i % tps, 0)))
            out_shape.append(jax.ShapeDtypeStruct((t // seq_len, dil, seq_len // dil, 3 * BRANCH), BF16))
        k_stack, v_stack, layer = keep
        skip = tps - k_stack.shape[2] // tm
        for stack in (k_stack, v_stack):
            aliases[len(args)] = len(out_shape)
            args.append(stack)
            in_specs.append(pl.BlockSpec(memory_space=pl.ANY))
            out_specs.append(pl.BlockSpec((None, None, tm, BRANCH),
                                          lambda i: (layer, i // tps, jnp.maximum(i % tps - skip, 0), 0)))
            out_shape.append(jax.ShapeDtypeStruct(stack.shape, stack.dtype))
        scratch = [pltpu.VMEM((3 * BRANCH // LANES, tm, LANES), F32)]
    return pl.pallas_call(
        functools.partial(_inproj_body, tm=tm, prompt=prompt),
        grid=(t // tm,),
        in_specs=in_specs,
        out_specs=out_specs,
        out_shape=out_shape,
        input_output_aliases=aliases,
        scratch_shapes=scratch,
        compiler_params=pltpu.CompilerParams(dimension_semantics=("arbitrary",), vmem_limit_bytes=VMEM_LIMIT),
        name="inproj",
    )(*args)


def _swa_body(q_ref, kp_ref, kc_ref, vp_ref, vc_ref, o_ref, lse_ref, *, dil, tq, n_sub):
    n = pl.program_id(1)
    q = (q_ref[...].astype(F32) * (HEAD_DIM ** -0.5)).astype(BF16)
    k = jnp.concatenate([kp_ref[...], kc_ref[...]], axis=0).astype(BF16)
    v = jnp.concatenate([vp_ref[...], vc_ref[...]], axis=0).astype(BF16)
    iq = lax.broadcasted_iota(jnp.int32, (tq, 2 * tq), 0)
    jk = lax.broadcasted_iota(jnp.int32, (tq, 2 * tq), 1)
    dist = iq + tq - jk
    in_band = (dist >= 0) & (dist <= ATTN_STEPS)
    bias = (dist * dil).astype(F32)
    head_bias = [ALIBI_SLOPES[h] * bias for h in range(N_HEADS)]
    first_valid = in_band & ((jk >= tq) | (n > 0))
    ones = jnp.ones((v.shape[0], HEAD_DIM), BF16)
    v_aug = [jnp.concatenate([v[:, h * HEAD_DIM:(h + 1) * HEAD_DIM], ones], axis=-1) for h in range(N_HEADS)]
    for i in range(n_sub):
        valid = first_valid if i == 0 else in_band
        rows = slice(i * tq, (i + 1) * tq)
        keys = slice(i * tq, (i + 2) * tq)
        outs, lses = [], []
        for h in range(N_HEADS):
            sl = slice(h * HEAD_DIM, (h + 1) * HEAD_DIM)
            s = lax.dot_general(q[rows, sl], k[keys, sl], (((1,), (1,)), ((), ())), preferred_element_type=F32)
            s = jnp.where(valid, s - head_bias[h], NEG_MASK)
            m = jnp.max(s, axis=-1, keepdims=True)
            p = jnp.exp(s - m)
            nd = jnp.dot(p.astype(BF16), v_aug[h][keys], preferred_element_type=F32)
            den = nd[:, HEAD_DIM:2 * HEAD_DIM]
            outs.append(nd[:, 0:HEAD_DIM] / den)
            lses.append(m + jnp.log(den))
        o_ref[rows, :] = jnp.concatenate(outs, axis=-1)
        lse_ref[rows, :] = jnp.concatenate(lses, axis=-1)


def _swa(qkv, dil, n_sub):
    b, _, n, _ = qkv.shape
    tq = ATTN_STEPS
    tqb = tq * n_sub

    def cur(col):
        return pl.BlockSpec((None, None, tqb, BRANCH), lambda zi, ni: (zi // dil, zi % dil, ni, col))

    def prev(col):
        return pl.BlockSpec((None, None, tq, BRANCH),
                            lambda zi, ni: (zi // dil, zi % dil, jnp.maximum(ni * n_sub - 1, 0), col))

    shp = jax.ShapeDtypeStruct((b, dil, n, BRANCH), F32)
    return pl.pallas_call(
        functools.partial(_swa_body, dil=dil, tq=tq, n_sub=n_sub),
        grid=(b * dil, n // tqb),
        in_specs=[cur(0), prev(1), cur(1), prev(2), cur(2)],
        out_specs=[cur(0), cur(0)],
        out_shape=[shp, shp],
        compiler_params=pltpu.CompilerParams(dimension_semantics=("arbitrary", "arbitrary"),
                                             vmem_limit_bytes=VMEM_LIMIT),
        name=f"swa_d{dil}",
    )(qkv, qkv, qkv, qkv, qkv)


def _merge_patterns(o_refs, lse_refs, scratch, tm):
    outs, lses = [], []
    si = 0
    for (_, dil), o_ref, lse_ref in zip(ATTN_PATTERNS, o_refs, lse_refs):
        if dil == 1:
            outs.append(o_ref[...])
            lses.append(lse_ref[...])
            continue
        ob, lb = scratch[si], scratch[si + 1]
        si += 2
        n_slab = BRANCH // LANES
        for r in range(dil):
            for c in range(n_slab):
                ob[c, pl.ds(r, tm // dil, stride=dil), :] = o_ref[r, :, c * LANES:(c + 1) * LANES]
                lb[c, pl.ds(r, tm // dil, stride=dil), :] = lse_ref[r, :, c * LANES:(c + 1) * LANES]
        outs.append(jnp.concatenate([ob[c] for c in range(n_slab)], axis=-1))
        lses.append(jnp.concatenate([lb[c] for c in range(n_slab)], axis=-1))
    l_all = lses[0]
    for l in lses[1:]:
        l_all = jnp.maximum(l_all, l)
    numer = jnp.zeros_like(l_all)
    denom = jnp.zeros_like(l_all)
    for o, l in zip(outs, lses):
        wgt = jnp.exp(l - l_all)
        numer = numer + wgt * o
        denom = denom + wgt
    return numer / denom


def _attn_sample_one(pa, kt, vt, t_new, buf):
    rows = N_HEADS * t_new
    q = pa[:, 0:BRANCH] * (HEAD_DIM ** -0.5)
    k_new = pa[:, BRANCH:2 * BRANCH]
    v_new = pa[:, 2 * BRANCH:3 * BRANCH]
    gate = pa[:, 3 * BRANCH:4 * BRANCH]
    row_h = lax.broadcasted_iota(jnp.int32, (rows, BRANCH), 0) // t_new
    lane_h = lax.broadcasted_iota(jnp.int32, (rows, BRANCH), 1) // HEAD_DIM
    own = row_h == lane_h
    q_bd = jnp.where(own, jnp.concatenate([q] * N_HEADS, axis=0), 0.0).astype(BF16)
    pad = jnp.zeros((LANES - t_new, BRANCH), F32)
    k_new_p = jnp.concatenate([k_new, pad], axis=0).astype(BF16)
    v_new_p = jnp.concatenate([v_new, pad], axis=0).astype(BF16)
    kt = kt.astype(BF16)
    vt = vt.astype(BF16)
    nt = (((1,), (1,)), ((), ()))
    s = jnp.concatenate([jnp.dot(q_bd, kt, preferred_element_type=F32),
                         lax.dot_general(q_bd, k_new_p, nt, preferred_element_type=F32)], axis=-1)
    ncol = buf + LANES
    col = lax.broadcasted_iota(jnp.int32, (rows, ncol), 1)
    row = lax.broadcasted_iota(jnp.int32, (rows, ncol), 0)
    delta = buf + row % t_new - col
    hrow = row // t_new
    slope = jnp.where(hrow == 0, ALIBI_SLOPES[0],
                      jnp.where(hrow == 1, ALIBI_SLOPES[1], jnp.where(hrow == 2, ALIBI_SLOPES[2], ALIBI_SLOPES[3])))
    s = s - slope * delta.astype(F32)
    ps, ms, dens = [], [], []
    for win, dil in ATTN_PATTERNS:
        valid = (delta >= 0) & (delta <= win) & ((delta & (dil - 1)) == 0)
        sp = jnp.where(valid, s, NEG_MASK)
        m = jnp.max(sp, axis=-1, keepdims=True)
        p = jnp.exp(sp - m)
        ps.append(p.astype(BF16))
        ms.append(m)
        dens.append(jnp.sum(p, axis=-1, keepdims=True))
    p_all = jnp.concatenate(ps, axis=0)
    num_all = (lax.dot_general(p_all[:, :buf], vt, nt, preferred_element_type=F32)
               + jnp.dot(p_all[:, buf:], v_new_p, preferred_element_type=F32))
    m_all = jnp.maximum(jnp.maximum(ms[0], ms[1]), ms[2])
    numer = jnp.zeros((rows, BRANCH), F32)
    denom = jnp.zeros((rows, 1), F32)
    for g in range(len(ATTN_PATTERNS)):
        wgt = jnp.exp(ms[g] - m_all)
        numer = numer + wgt * num_all[g * rows:(g + 1) * rows]
        denom = denom + wgt * dens[g]
    o_full = jnp.where(own, numer / denom, 0.0)
    o = o_full[0:t_new]
    for h in range(1, N_HEADS):
        o = o + o_full[h * t_new:(h + 1) * t_new]
    return o * _silu(gate)


def _attn_sample_body(pa_ref, kt_ref, vt_ref, out_ref, *, t_new, buf, nb):
    for j in range(nb):
        out_ref[j] = _attn_sample_one(pa_ref[j], kt_ref[j], vt_ref[j], t_new, buf)


def _delta_body(pb_ref, pe_ref, cpre_ref, s0_ref, cw_ref, alog_ref, dtb_ref, nw_ref, _stack_ref, y_ref, s_out_ref,
                xbuf, s_scr, *, ns, tl, chunk, n_tiles):
    t = pl.program_id(1)
    nc = tl // chunk
    nb = ns * nc
    nbh = nb * N_HEADS
    rows = ns * tl
    width = 3 * BRANCH

    @pl.when(t == 0)
    def _():
        xbuf[:, 0:SUBLANES, :] = cpre_ref[...]
        s_scr[...] = s0_ref[...]

    xbuf[:, SUBLANES:SUBLANES + tl, :] = pb_ref[:, :, 0:width]
    conv = jnp.zeros((ns, tl, width), F32)
    for tap in range(CONV_TAPS):
        off = SUBLANES - (CONV_TAPS - 1) + tap
        conv = conv + xbuf[:, off:off + tl, :] * cw_ref[tap:tap + 1, :]
    if n_tiles > 1:
        xbuf[:, 0:SUBLANES, :] = xbuf[:, tl:tl + SUBLANES, :]
    conv = _silu(conv).reshape(rows, width)
    g_ones = _head_ones()
    q = conv[:, 0:BRANCH]
    k = conv[:, BRANCH:2 * BRANCH]
    v = conv[:, 2 * BRANCH:3 * BRANCH]
    q = q * lax.rsqrt(_head_sum(q * q, g_ones) + 1e-6) * (HEAD_DIM ** -0.5)
    k = k * lax.rsqrt(_head_sum(k * k, g_ones) + 1e-6)
    pe = pe_ref[...].reshape(rows, LANES)
    lane = lax.broadcasted_iota(jnp.int32, (rows, LANES), 1)
    gb = jnp.where(lane < N_HEADS, _sigmoid(pe), -jnp.exp(alog_ref[...]) * _softplus(pe + dtb_ref[...]))

    qh = _to_head_batch(q, nb, chunk)
    kh = _to_head_batch(k, nb, chunk)
    vh = _to_head_batch(v, nb, chunk)

    def col_batch(first):
        cols = [gb[:, first + h:first + h + 1].reshape(nb, 1, chunk, 1) for h in range(N_HEADS)]
        return jnp.concatenate(cols, axis=1).reshape(nbh, chunk, 1)

    beta = col_batch(0)
    g = col_batch(N_HEADS)
    ri = lax.broadcasted_iota(jnp.int32, (nbh, chunk, chunk), 1)
    ci = lax.broadcasted_iota(jnp.int32, (nbh, chunk, chunk), 2)
    ltri = jnp.where(ri >= ci, 1.0, 0.0).astype(BF16)
    eye = jnp.where(ri == ci, 1.0, 0.0).astype(F32)
    gm = jnp.where(ri > ci, jnp.broadcast_to(g, (nbh, chunk, chunk)), 0.0)
    dmat = sum(_bmm(ltri, term) for term in _split3(gm))
    gc = dmat[:, :, 0:1] + g[:, 0:1, :]
    g_last = gc[:, chunk - 1:chunk, :]
    decay = jnp.where(ri >= ci, jnp.exp(dmat), 0.0)
    kb = kh * beta
    vb = vh * beta
    kh_b = kh.astype(BF16)
    a_mat = jnp.where(ri > ci, _bmm_nt(kb.astype(BF16), kh_b) * decay, 0.0)
    n_fac = int(math.log2(chunk))
    x = -a_mat
    u = eye + x
    x = _bmm1(x, x)
    for _ in range(2, n_fac):
        prod = _bmm1(jnp.concatenate([x, u], axis=1), x)
        x = prod[:, 0:chunk]
        u = u + prod[:, chunk:2 * chunk]
    t_inv = u + _bmm1(u, x)
    uw = _bmm3(t_inv, jnp.concatenate([vb, kb * jnp.exp(gc)], axis=-1))
    u_mat = uw[:, :, 0:HEAD_DIM]
    w_mat = uw[:, :, HEAD_DIM:2 * HEAD_DIM]
    qk = _bmm_nt(qh.astype(BF16), kh_b) * decay
    qg = qh * jnp.exp(gc)
    kg = kh * jnp.exp(g_last - gc)
    e_last = jnp.exp(g_last)

    def sel(a, c):
        a5 = a.reshape((ns, nc, N_HEADS) + a.shape[1:])
        return a5[:, c].reshape((ns * N_HEADS,) + a.shape[1:])

    st = s_scr[...].reshape(ns * N_HEADS, HEAD_DIM, HEAD_DIM)
    o_chunks = []
    for c in range(nc):
        s_b = st.astype(BF16)
        wq = _bmm(jnp.concatenate([sel(w_mat, c), sel(qg, c)], axis=1).astype(BF16), s_b)
        v_new = (sel(u_mat, c) - wq[:, 0:chunk]).astype(BF16)
        o_chunks.append(wq[:, chunk:2 * chunk] + _bmm(sel(qk, c).astype(BF16), v_new))
        st = st * sel(e_last, c) + _bmm_tn(sel(kg, c).astype(BF16), v_new)
    s_scr[...] = st.reshape(ns, N_HEADS, HEAD_DIM, HEAD_DIM)
    if nc > 1:
        o_all = jnp.concatenate([oc.reshape(ns, 1, N_HEADS, chunk, HEAD_DIM) for oc in o_chunks], axis=1)
        o_all = o_all.reshape(nbh, chunk, HEAD_DIM)
    else:
        o_all = o_chunks[0]
    o = _from_head_batch(o_all, nb, chunk)
    o = o * lax.rsqrt(_head_sum(o * o, g_ones) * (1.0 / HEAD_DIM) + NORM_EPS) * nw_ref[...]
    gate = pb_ref[:, :, width:width + BRANCH].reshape(rows, BRANCH)
    y_ref[...] = (o * _silu(gate)).reshape(ns, tl, BRANCH)

    @pl.when(t == n_tiles - 1)
    def _():
        s_out_ref[...] = s_scr[...]


def _state_spec(ns, layer):
    return pl.BlockSpec((None, ns, N_HEADS, HEAD_DIM, HEAD_DIM), lambda bi, ti: (layer, bi, 0, 0, 0))


def _delta(pb, pe, cpre, s0, s0_layer, s_stack, layer, cw, alog_row, dtb_row, nw_row, *, ns, tl, chunk):
    b, l, _ = pb.shape
    n_tiles = l // tl
    seq3 = lambda w: pl.BlockSpec((ns, tl, w), lambda bi, ti: (bi, ti, 0))
    const2 = lambda a: pl.BlockSpec(a.shape, lambda bi, ti: (0, 0))
    return pl.pallas_call(
        functools.partial(_delta_body, ns=ns, tl=tl, chunk=chunk, n_tiles=n_tiles),
        grid=(b // ns, n_tiles),
        in_specs=[seq3(4 * BRANCH), seq3(LANES),
                  pl.BlockSpec((ns, SUBLANES, 3 * BRANCH), lambda bi, ti: (bi, 0, 0)), _state_spec(ns, s0_layer),
                  const2(cw), const2(alog_row), const2(dtb_row), const2(nw_row),
                  pl.BlockSpec(memory_space=pl.ANY)],
        out_specs=[seq3(BRANCH), _state_spec(ns, layer)],
        out_shape=[jax.ShapeDtypeStruct((b, l, BRANCH), F32),
                   jax.ShapeDtypeStruct(s_stack.shape, F32)],
        input_output_aliases={8: 1},
        scratch_shapes=[pltpu.VMEM((ns, SUBLANES + tl, 3 * BRANCH), F32),
                        pltpu.VMEM((ns, N_HEADS, HEAD_DIM, HEAD_DIM), F32)],
        compiler_params=pltpu.CompilerParams(dimension_semantics=("arbitrary", "arbitrary"),
                                             vmem_limit_bytes=VMEM_LIMIT),
        name="delta",
    )(pb, pe, cpre, s0, cw, alog_row, dtb_row, nw_row, s_stack)


def _hgrn_body(pc_ref, lbraw_ref, s0_ref, nw_ref, _stack_ref, y_ref, s_out_ref, s_scr,
               *, ns, tl, chunk, n_tiles, layer):
    t = pl.program_id(1)
    nc = tl // chunk
    nb = ns * nc
    rows = ns * tl

    def flip(s4):
        return jnp.swapaxes(s4.reshape(ns * N_HEADS, HEAD_DIM, HEAD_DIM), 1, 2).reshape(s4.shape)

    @pl.when(t == 0)
    def _():
        s_scr[...] = flip(s0_ref[...])

    raw = lbraw_ref[...]
    e = jnp.exp(raw - jnp.max(raw, axis=0, keepdims=True))
    sm = e / jnp.sum(e, axis=0, keepdims=True)
    lb = jnp.zeros((1, BRANCH), F32)
    for d in range(1, layer + 1):
        lb = lb + sm[d:d + 1, :]

    pc = pc_ref[...].reshape(rows, 4 * BRANCH)
    qh = _silu(pc[:, 0:BRANCH])
    fr = pc[:, BRANCH:2 * BRANCH]
    vh = pc[:, 2 * BRANCH:3 * BRANCH]
    gate = pc[:, 3 * BRANCH:4 * BRANCH]
    f = lb + (1.0 - lb) * _sigmoid(fr)
    log_f = jnp.log(f).reshape(nb, chunk, BRANCH)
    kh = (1.0 - lb) * _sigmoid(-fr)

    q3 = qh.reshape(nb, chunk, BRANCH)
    k3 = kh.reshape(nb, chunk, BRANCH)
    v3 = vh.reshape(nb, chunk, BRANCH)
    ri = lax.broadcasted_iota(jnp.int32, (nb, chunk, chunk), 1)
    ci = lax.broadcasted_iota(jnp.int32, (nb, chunk, chunk), 2)
    ltri = jnp.where(ri >= ci, 1.0, 0.0).astype(BF16)
    l1 = log_f.astype(BF16)
    r1 = log_f - l1.astype(F32)
    l2 = r1.astype(BF16)
    l3 = (r1 - l2.astype(F32)).astype(BF16)
    gcum = _bmm(ltri, l1) + _bmm(ltri, l2) + _bmm(ltri, l3)
    g_last = gcum[:, chunk - 1:chunk, :]
    qg = q3 * jnp.exp(gcum)
    kg = k3 * jnp.exp(g_last - gcum)
    e_last = jnp.exp(g_last)

    g_ones = _head_ones()
    n_grp = chunk // SUBLANES
    row_i = [g * SUBLANES + lax.broadcasted_iota(jnp.int32, (nb, chunk - g * SUBLANES, BRANCH), 1)
             for g in range(n_grp)]
    acc = [jnp.zeros((nb, SUBLANES, BRANCH), F32) for _ in range(n_grp)]
    for j in range(chunk):
        g0 = j // SUBLANES
        lo = g0 * SUBLANES
        dec = jnp.exp(jnp.where(row_i[g0] >= j, gcum[:, lo:, :] - gcum[:, j:j + 1, :], NEG_MASK))
        t_j = (q3[:, lo:, :] * dec * k3[:, j:j + 1, :]).reshape(nb * (chunk - lo), BRANCH)
        a_j = jnp.dot(t_j.astype(BF16), g_ones, preferred_element_type=F32).reshape(nb, chunk - lo, BRANCH)
        c_j = a_j * v3[:, j:j + 1, :]
        for gi in range(g0, n_grp):
            acc[gi] = acc[gi] + c_j[:, (gi - g0) * SUBLANES:(gi - g0 + 1) * SUBLANES, :]
    intra = jnp.concatenate(acc, axis=1) if n_grp > 1 else acc[0]

    qg_b = _to_head_batch(qg.reshape(rows, BRANCH), nb, chunk).astype(BF16)
    kg_b = _to_head_batch(kg.reshape(rows, BRANCH), nb, chunk).astype(BF16)
    v_b = _to_head_batch(vh, nb, chunk).astype(BF16)
    el_b = _to_head_batch(e_last.reshape(nb, BRANCH), nb, 1)

    def sel(a, c):
        a5 = a.reshape((ns, nc, N_HEADS) + a.shape[1:])
        return a5[:, c].reshape((ns * N_HEADS,) + a.shape[1:])

    st = s_scr[...].reshape(ns * N_HEADS, HEAD_DIM, HEAD_DIM)
    o_chunks = []
    for c in range(nc):
        o_chunks.append(_bmm_nt(sel(qg_b, c), st.astype(BF16)))
        st = st * sel(el_b, c) + _bmm_tn(sel(v_b, c), sel(kg_b, c))
    s_scr[...] = st.reshape(ns, N_HEADS, HEAD_DIM, HEAD_DIM)
    if nc > 1:
        o_all = jnp.concatenate([oc.reshape(ns, 1, N_HEADS, chunk, HEAD_DIM) for oc in o_chunks], axis=1)
        o_all = o_all.reshape(nb * N_HEADS, chunk, HEAD_DIM)
    else:
        o_all = o_chunks[0]
    o = _from_head_batch(o_all, nb, chunk) + intra.reshape(rows, BRANCH)
    o = o * lax.rsqrt(_head_sum(o * o, g_ones) * (1.0 / HEAD_DIM) + NORM_EPS) * nw_ref[...]
    y_ref[...] = (o * _silu(gate)).reshape(ns, tl, BRANCH)

    @pl.when(t == n_tiles - 1)
    def _():
        s_out_ref[...] = flip(s_scr[...])


def _hgrn_attn_body(pc_ref, lbraw_ref, s0_ref, nw_ref, stack_ref, pa_ref, kt_ref, vt_ref, y_ref, s_out_ref, ya_ref,
                    s_scr, *, hgrn_args, attn_args):
    _hgrn_body(pc_ref, lbraw_ref, s0_ref, nw_ref, stack_ref, y_ref, s_out_ref, s_scr, **hgrn_args)
    _attn_sample_body(pa_ref, kt_ref, vt_ref, ya_ref, **attn_args)


def _hgrn(pc, lb_raw, s0, s0_layer, s_stack, nw_row, *, ns, tl, chunk, layer, rider=None):
    b, l, _ = pc.shape
    n_tiles = l // tl
    hgrn_args = dict(ns=ns, tl=tl, chunk=chunk, n_tiles=n_tiles, layer=layer)
    in_specs = [pl.BlockSpec((ns, tl, 4 * BRANCH), lambda bi, ti: (bi, ti, 0)),
                pl.BlockSpec(lb_raw.shape, lambda bi, ti: (0, 0)), _state_spec(ns, s0_layer),
                pl.BlockSpec(nw_row.shape, lambda bi, ti: (0, 0)),
                pl.BlockSpec(memory_space=pl.ANY)]
    out_specs = [pl.BlockSpec((ns, tl, BRANCH), lambda bi, ti: (bi, ti, 0)), _state_spec(ns, layer)]
    out_shape = [jax.ShapeDtypeStruct((b, l, BRANCH), F32), jax.ShapeDtypeStruct(s_stack.shape, F32)]
    args = [pc, lb_raw, s0, nw_row, s_stack]
    body = functools.partial(_hgrn_body, **hgrn_args)
    if rider is not None:
        pa_s, cache_kt, cache_vt = rider
        assert b == ns and pa_s.shape[0] % n_tiles == 0
        nb = pa_s.shape[0] // n_tiles
        t_new, buf = pa_s.shape[1], cache_kt.shape[3]
        cache = pl.BlockSpec((None, nb, BRANCH, buf), lambda bi, ti: (layer, ti, 0, 0))
        in_specs += [pl.BlockSpec((nb, t_new, 4 * BRANCH), lambda bi, ti: (ti, 0, 0)), cache, cache]
        out_specs.append(pl.BlockSpec((nb, t_new, BRANCH), lambda bi, ti: (ti, 0, 0)))
        out_shape.append(jax.ShapeDtypeStruct((pa_s.shape[0], t_new, BRANCH), F32))
        args += [pa_s, cache_kt, cache_vt]
        body = functools.partial(_hgrn_attn_body, hgrn_args=hgrn_args, attn_args=dict(t_new=t_new, buf=buf, nb=nb))
    return pl.pallas_call(
        body,
        grid=(b // ns, n_tiles),
        in_specs=in_specs,
        out_specs=out_specs,
        out_shape=out_shape,
        input_output_aliases={4: 1},
        scratch_shapes=[pltpu.VMEM((ns, N_HEADS, HEAD_DIM, HEAD_DIM), F32)],
        compiler_params=pltpu.CompilerParams(dimension_semantics=("arbitrary", "arbitrary"),
                                             vmem_limit_bytes=VMEM_LIMIT),
        name="hgrn" if rider is None else "hgrn_attn",
    )(*args)


def _post_body(*refs, ns, tl, n_tiles, start_pos, final, n_attn):
    attn_refs = refs[0:n_attn]
    (x_ref, yb_ref, yc_ref, pd_ref, ppre_ref, pw_ref, ps_ref, wo_ref, fw_ref, out_ref, xbuf) = refs[n_attn:n_attn + 11]
    slabs = refs[n_attn + 11:]
    t = pl.program_id(1)
    rows = ns * tl
    if n_attn == 1:
        ya = attn_refs[0][...].reshape(rows, BRANCH)
    else:
        n_g = len(ATTN_PATTERNS)
        ya = _merge_patterns(attn_refs[0:n_g], attn_refs[n_g:2 * n_g], slabs, tl) * _silu(attn_refs[2 * n_g][...])

    @pl.when(t == 0)
    def _():
        xbuf[:, 0:POOL_MAX, :] = ppre_ref[...]

    xbuf[:, POOL_MAX:POOL_MAX + tl, :] = pd_ref[:, :, 0:BRANCH]

    def back(kk):
        return xbuf[:, POOL_MAX - kk:POOL_MAX - kk + tl, :]

    x0 = back(0)
    sums = {}
    acc = x0
    for kk in range(1, POOL_MAX):
        acc = acc + back(kk)
        if kk + 1 in POOL_WINDOWS:
            sums[kk + 1] = acc
    if n_tiles > 1:
        xbuf[:, 0:POOL_MAX, :] = xbuf[:, tl:tl + POOL_MAX, :]
    group = lax.broadcasted_iota(jnp.int32, (ns, tl, BRANCH), 2) // HEAD_DIM
    pos = start_pos + t * tl + lax.broadcasted_iota(jnp.int32, (ns, tl, BRANCH), 1)
    tot = sums[POOL_WINDOWS[-1]]
    win = jnp.full((ns, tl, BRANCH), POOL_WINDOWS[-1], jnp.int32)
    for gi in range(len(POOL_WINDOWS) - 2, -1, -1):
        tot = jnp.where(group == gi, sums[POOL_WINDOWS[gi]], tot)
        win = jnp.where(group == gi, POOL_WINDOWS[gi], win)
    cnt = jnp.minimum(pos + 1, win).astype(F32)
    pooled = (tot / cnt - x0).reshape(rows, BRANCH)
    gate_d = pd_ref[:, :, BRANCH:2 * BRANCH].reshape(rows, BRANCH)
    yd = jnp.dot(pooled.astype(BF16), pw_ref[...], preferred_element_type=F32) * ps_ref[...] * _silu(gate_d)
    ycat = jnp.concatenate([ya, yb_ref[...].reshape(rows, BRANCH),
                            yc_ref[...].reshape(rows, BRANCH), yd], axis=-1).astype(BF16)
    x_new = x_ref[...].reshape(rows, -1) + jnp.dot(ycat, wo_ref[...], preferred_element_type=F32)
    if final:
        ms = jnp.mean(x_new * x_new, axis=-1, keepdims=True)
        x_new = x_new * lax.rsqrt(ms + NORM_EPS) * fw_ref[...]
    out_ref[...] = x_new.reshape(out_ref.shape)


def _post(attn, x, yb, yc, pd, ppre, pw_bd, ps_row, wo, fw_row, *, ns, tl, start_pos, final):
    b, l, d = x.shape
    n_tiles = l // tl
    seq3 = lambda w: pl.BlockSpec((ns, tl, w), lambda bi, ti: (bi, ti, 0))
    const2 = lambda a: pl.BlockSpec(a.shape, lambda bi, ti: (0, 0))
    if len(attn) == 1:
        attn_specs, slabs = [seq3(BRANCH)], []
    else:
        assert ns == 1

        def cls(dil):
            if dil == 1:
                return pl.BlockSpec((None, None, tl, BRANCH), lambda bi, ti: (bi, 0, ti, 0))
            return pl.BlockSpec((None, dil, tl // dil, BRANCH), lambda bi, ti: (bi, 0, ti, 0))

        attn_specs = 2 * [cls(dil) for _, dil in ATTN_PATTERNS] + [
            pl.BlockSpec((None, tl, BRANCH), lambda bi, ti: (bi, ti, 3))]
        slabs = [pltpu.VMEM((BRANCH // LANES, tl, LANES), F32)] * (2 * len(DILATED))
    return pl.pallas_call(
        functools.partial(_post_body, ns=ns, tl=tl, n_tiles=n_tiles, start_pos=start_pos, final=final,
                          n_attn=len(attn)),
        grid=(b // ns, n_tiles),
        in_specs=attn_specs + [seq3(d), seq3(BRANCH), seq3(BRANCH), seq3(2 * BRANCH),
                               pl.BlockSpec((ns, POOL_MAX, BRANCH), lambda bi, ti: (bi, 0, 0)),
                               const2(pw_bd), const2(ps_row), const2(wo), const2(fw_row)],
        out_specs=seq3(d),
        out_shape=jax.ShapeDtypeStruct((b, l, d), F32),
        scratch_shapes=[pltpu.VMEM((ns, POOL_MAX + tl, BRANCH), F32)] + slabs,
        compiler_params=pltpu.CompilerParams(dimension_semantics=("arbitrary", "arbitrary"),
                                             vmem_limit_bytes=VMEM_LIMIT),
        name="post",
    )(*attn, x, yb, yc, pd, ppre, pw_bd, ps_row, wo, fw_row)


def _prep_weights(w_in, w_out, pool_w):
    n_small = 2 * N_HEADS
    main = 8 * BRANCH
    w_perm = (w_in[:, :, 0:main].astype(BF16), w_in[:, :, main + n_small:].astype(BF16),
              jnp.pad(w_in[:, :, main:main + n_small], ((0, 0), (0, 0), (0, LANES - n_small))).astype(BF16))
    depth, groups = pool_w.shape[0:2]
    pw_bd = jnp.zeros((depth, BRANCH, BRANCH), pool_w.dtype)
    for g in range(groups):
        sl = slice(g * HEAD_DIM, (g + 1) * HEAD_DIM)
        pw_bd = pw_bd.at[:, sl, sl].set(pool_w[:, g])
    return w_perm, w_out.astype(BF16), pw_bd.astype(BF16)


def _attn_prompt(pa3, classes, max_sub):
    b, l, w = pa3.shape
    os_, lses = [], []
    ci = 0
    for _, dil in ATTN_PATTERNS:
        n_sub = min(max_sub, l // dil // ATTN_STEPS)
        if dil == 1:
            o, lse = _swa(pa3.reshape(b, 1, l, w), 1, n_sub)
        else:
            o, lse = _swa(classes[ci], dil, n_sub)
            ci += 1
        os_.append(o)
        lses.append(lse)
    return os_ + lses + [pa3]


def _trunk(x, start_pos, states, prm, cfg):
    b, l, d = x.shape
    depth = prm['norm_w'].shape[0]
    dt = x.dtype
    ns, tl = cfg['ns'], cfg['tl']
    prompt = states is None
    state_shape = (depth, b, N_HEADS, HEAD_DIM, HEAD_DIM)
    d_stack = jnp.zeros(state_shape, F32)
    h_stack = jnp.zeros(state_shape, F32)
    if prompt:
        keep = min(ATTN_PATTERNS[-1][0], l)
        k_stack = jnp.zeros((depth, b, keep, BRANCH), F32)
        v_stack = jnp.zeros((depth, b, keep, BRANCH), F32)
        d_in = h_in = jnp.zeros((1,) + state_shape[1:], F32)
    else:
        c_kt, c_vt, d_in, c_conv, h_in, c_pool = states
    ks, vs, dcs, ps = [], [], [], []
    for layer in range(depth):
        if prompt:
            conv_prefix = jnp.zeros((b, CONV_TAPS - 1, 3 * BRANCH), dt)
            pool_prefix = jnp.zeros((b, POOL_MAX - 1, BRANCH), dt)
        else:
            conv_prefix, pool_prefix = c_conv[layer], c_pool[layer]
        outs = _inproj(x.reshape(b * l, d), prm['norm_w'][layer][None, :], tuple(wi[layer] for wi in prm['w_in']), cfg['tm'],
                       seq_len=l if prompt else None, keep=(k_stack, v_stack, layer) if prompt else None)
        pa, pb, pc, pd, pe = outs[0:5]
        pa3 = pa.reshape(b, l, 4 * BRANCH)
        s0_layer = 0 if prompt else layer
        hgrn_call = functools.partial(
            _hgrn, pc.reshape(b, l, 4 * BRANCH), prm['hgrn_lb_raw'], h_in, s0_layer, h_stack,
            prm['hgrn_nw_row'][layer], ns=ns, tl=cfg['tl_hgrn'], chunk=math.gcd(l, HGRN_CHUNK), layer=layer)
        if prompt:
            n_cls = len(DILATED)
            attn = _attn_prompt(pa3, outs[5:5 + n_cls], cfg['swa_sub'])
            k_stack, v_stack = outs[5 + n_cls:]
            yc, h_stack = yield hgrn_call
        else:
            attn = [(yield pa3)]
            ks.append(pa3[:, :, BRANCH:2 * BRANCH].reshape(b, l, N_HEADS, HEAD_DIM))
            vs.append(pa3[:, :, 2 * BRANCH:3 * BRANCH].reshape(b, l, N_HEADS, HEAD_DIM))
            yc, h_stack = hgrn_call()

        pb3 = pb.reshape(b, l, 4 * BRANCH)
        cpre = jnp.pad(conv_prefix, ((0, 0), (SUBLANES - (CONV_TAPS - 1), 0), (0, 0)))
        yb, d_stack = _delta(pb3, pe.reshape(b, l, LANES), cpre, d_in, s0_layer, d_stack, layer,
                             prm['delta_conv_w'][layer], prm['alog_row'][layer], prm['dtb_row'][layer],
                             prm['delta_nw_row'][layer], ns=ns, tl=tl, chunk=math.gcd(l, DELTA_CHUNK))
        dcs.append(jnp.concatenate([conv_prefix, pb3[:, :, 0:3 * BRANCH]], axis=1)[:, -(CONV_TAPS - 1):])

        pd3 = pd.reshape(b, l, 2 * BRANCH)
        ppre = jnp.pad(pool_prefix, ((0, 0), (1, 0), (0, 0)))
        ps.append(jnp.concatenate([pool_prefix, pd3[:, :, 0:BRANCH]], axis=1)[:, -(POOL_MAX - 1):])
        x = _post(attn, x, yb, yc, pd3, ppre, prm['pool_w_bd'][layer], prm['pool_scale'][layer][None, :],
                  prm['w_out'][layer], prm['final_norm_w'][None, :], ns=cfg['ns_post'], tl=cfg['tl_post'],
                  start_pos=start_pos, final=(layer == depth - 1))
    if prompt:
        k_all = k_stack.reshape(depth, b, keep, N_HEADS, HEAD_DIM)
        v_all = v_stack.reshape(depth, b, keep, N_HEADS, HEAD_DIM)
    else:
        k_all, v_all = jnp.stack(ks), jnp.stack(vs)
    return x, (k_all, v_all, d_stack, jnp.stack(dcs), h_stack, jnp.stack(ps))


def _lane_row(vals, offset):
    depth, n = vals.shape
    return jnp.zeros((depth, 1, LANES), F32).at[:, 0, offset:offset + n].set(vals.astype(F32))


def kernel(x_prompt, x_sample, cache_attn_k, cache_attn_v, state_delta, state_delta_conv, state_hgrn, state_pool,
           norm_w, w_in, w_out, delta_conv_w, delta_a_log, delta_dt_bias, delta_norm_w, hgrn_lb_raw, hgrn_norm_w,
           pool_w, pool_scale, final_norm_w):
    w_in_p, w_out_b, pw_bd = _prep_weights(w_in, w_out, pool_w)
    prm = dict(
        norm_w=norm_w, w_in=w_in_p, w_out=w_out_b, pool_w_bd=pw_bd, pool_scale=pool_scale,
        final_norm_w=final_norm_w, delta_conv_w=delta_conv_w, hgrn_lb_raw=hgrn_lb_raw,
        alog_row=_lane_row(delta_a_log, N_HEADS), dtb_row=_lane_row(delta_dt_bias, N_HEADS),
        delta_nw_row=jnp.tile(delta_norm_w, (1, N_HEADS))[:, None, :],
        hgrn_nw_row=jnp.tile(hgrn_norm_w, (1, N_HEADS))[:, None, :],
    )
    bp = x_prompt.shape[0]
    dec_b, dec_l = x_sample.shape[0], x_sample.shape[1]
    depth, _, buf = cache_attn_k.shape[0:3]
    ckt = jnp.transpose(cache_attn_k, (0, 1, 3, 4, 2)).reshape(depth, dec_b, BRANCH, buf)
    cvt = jnp.transpose(cache_attn_v, (0, 1, 3, 4, 2)).reshape(depth, dec_b, BRANCH, buf)
    run_p = _trunk(x_prompt, 0, None, prm,
                   dict(tm=512, ns=bp, tl=256, tl_hgrn=128, ns_post=1, tl_post=1024, swa_sub=4))
    run_s = _trunk(x_sample, PAST_LEN, (ckt, cvt, state_delta, state_delta_conv, state_hgrn, state_pool), prm,
                   dict(tm=256, ns=32, tl=dec_l, tl_hgrn=dec_l, ns_post=32, tl_post=dec_l))
    hgrn_call, pa_s = next(run_p), next(run_s)
    for _ in range(depth):
        yc, h_stack, ya_s = hgrn_call(rider=(pa_s, ckt, cvt))
        try:
            hgrn_call = run_p.send((yc, h_stack))
        except StopIteration as done:
            y_p, st_p = done.value
        try:
            pa_s = run_s.send(ya_s)
        except StopIteration as done:
            y_s, st_s = done.value
    k_p, v_p, d_p, dc_p, h_p, pl_p = st_p
    k_s, v_s, d_s, dc_s, h_s, pl_s = st_s
    return (y_p, y_s, k_p, k_s, v_p, v_s, d_p, d_s, dc_p, dc_s, h_p, h_s, pl_p, pl_s)
```

```python
import functools
import math

import jax
import jax.numpy as jnp
from jax import lax
from jax.experimental import pallas as pl
from jax.experimental.pallas import tpu as pltpu

F32 = jnp.float32
BF16 = jnp.bfloat16

N_HEADS = 4
HEAD_DIM = 64
BRANCH = N_HEADS * HEAD_DIM
ATTN_PATTERNS = ((128, 1), (512, 4), (2048, 16))
ATTN_STEPS = 128
CONV_TAPS = 4
DELTA_CHUNK = 64
HGRN_CHUNK = 16
POOL_WINDOWS = (2, 4, 8, 16)
POOL_MAX = 16
PAST_LEN = 2048
NORM_EPS = 1e-6
NEG_MASK = -1e30
ALIBI_SLOPES = tuple(2.0 ** (-8.0 * (h + 1) / N_HEADS) for h in range(N_HEADS))

SUBLANES = 8
LANES = 128
VMEM_LIMIT = 56 * 1024 * 1024

P_WIDTHS = (4 * BRANCH, 4 * BRANCH, 4 * BRANCH, 2 * BRANCH, LANES)
DILATED = tuple(d for _, d in ATTN_PATTERNS if d > 1)


def _sigmoid(x):
    return 1.0 / (1.0 + jnp.exp(-x))


def _silu(x):
    return x * _sigmoid(x)


def _softplus(x):
    return jnp.maximum(x, 0.0) + jnp.log(1.0 + jnp.exp(-jnp.abs(x)))


def _head_ones():
    r = lax.broadcasted_iota(jnp.int32, (BRANCH, BRANCH), 0) // HEAD_DIM
    c = lax.broadcasted_iota(jnp.int32, (BRANCH, BRANCH), 1) // HEAD_DIM
    return jnp.where(r == c, 1.0, 0.0).astype(BF16)


def _split2(x):
    hi = x.astype(BF16)
    return hi, (x - hi.astype(F32)).astype(BF16)


def _split3(x):
    t1 = x.astype(BF16)
    r1 = x - t1.astype(F32)
    t2 = r1.astype(BF16)
    return t1, t2, (r1 - t2.astype(F32)).astype(BF16)


def _head_sum(x, g):
    hi, lo = _split2(x)
    return jnp.dot(hi, g, preferred_element_type=F32) + jnp.dot(lo, g, preferred_element_type=F32)


def _bmm(a, b):
    return jnp.einsum('bij,bjk->bik', a, b, preferred_element_type=F32)


def _bmm_nt(a, b):
    return jnp.einsum('bid,bjd->bij', a, b, preferred_element_type=F32)


def _bmm_tn(a, b):
    return jnp.einsum('bci,bcj->bij', a, b, preferred_element_type=F32)


def _bmm1(a, b):
    return _bmm(a.astype(BF16), b.astype(BF16))


def _bmm3(a, b):
    ah, al = _split2(a)
    bh, bl = _split2(b)
    return _bmm(ah, bh) + _bmm(ah, bl) + _bmm(al, bh)


def _to_head_batch(x2d, nb, chunk):
    parts = [x2d[:, h * HEAD_DIM:(h + 1) * HEAD_DIM].reshape(nb, 1, chunk, HEAD_DIM) for h in range(N_HEADS)]
    return jnp.concatenate(parts, axis=1).reshape(nb * N_HEADS, chunk, HEAD_DIM)


def _from_head_batch(x, nb, chunk):
    x4 = x.reshape(nb, N_HEADS, chunk, HEAD_DIM)
    return jnp.concatenate([x4[:, h].reshape(nb * chunk, HEAD_DIM) for h in range(N_HEADS)], axis=-1)


def _inproj_body(*refs, tm, prompt, n_stack_in, layer):
    x_ref, nw_ref, wab_ref, wcd_ref, we_ref = refs[0:5]
    first_out = 5 + n_stack_in
    pa_ref, pb_ref, pc_ref, pd_ref, pe_ref = refs[first_out:first_out + 5]
    x = x_ref[...]
    ms = jnp.mean(x * x, axis=-1, keepdims=True)
    h = (x * lax.rsqrt(ms + NORM_EPS) * nw_ref[...]).astype(BF16)
    wide = 4 * BRANCH
    pb_ref[...] = jnp.dot(h, wab_ref[:, wide:2 * wide], preferred_element_type=F32)
    pc_ref[...] = jnp.dot(h, wcd_ref[:, 0:wide], preferred_element_type=F32)
    pd_ref[...] = jnp.dot(h, wcd_ref[:, wide:wide + 2 * BRANCH], preferred_element_type=F32)
    pe_ref[...] = jnp.dot(h, we_ref[...], preferred_element_type=F32)
    p_a = jnp.dot(h, wab_ref[:, 0:wide], preferred_element_type=F32)
    pa_ref[...] = p_a
    if not prompt:
        return
    class_refs = refs[first_out + 5:first_out + 5 + len(DILATED)]
    k_keep, v_keep, slab = refs[first_out + 5 + len(DILATED):]
    if n_stack_in:
        k_keep[...] = p_a[:, BRANCH:2 * BRANCH]
        v_keep[...] = p_a[:, 2 * BRANCH:3 * BRANCH]
    else:
        k_keep[...] = jnp.zeros(k_keep.shape, F32)
        v_keep[...] = jnp.zeros(v_keep.shape, F32)
        k_keep[layer] = p_a[:, BRANCH:2 * BRANCH]
        v_keep[layer] = p_a[:, 2 * BRANCH:3 * BRANCH]
    n_slab = 3 * BRANCH // LANES
    for c in range(n_slab):
        slab[c] = p_a[:, c * LANES:(c + 1) * LANES]
    for ref, dil in zip(class_refs, DILATED):
        for r in range(dil):
            for c in range(n_slab):
                ref[0, r, :, c * LANES:(c + 1) * LANES] = slab[c, pl.ds(r, tm // dil, stride=dil), :].astype(BF16)


def _inproj(x2d, nw, w, tm, seq_len=None, keep=None):
    t, d = x2d.shape
    widths = P_WIDTHS
    out_specs = [pl.BlockSpec((tm, wd), lambda i: (i, 0)) for wd in widths]
    out_shape = [jax.ShapeDtypeStruct((t, wd), F32) for wd in widths]
    in_specs = [pl.BlockSpec((tm, d), lambda i: (i, 0)),
                pl.BlockSpec((1, d), lambda i: (0, 0))] + [pl.BlockSpec(wi.shape, lambda i: (0, 0)) for wi in w]
    args = [x2d, nw, *w]
    aliases, scratch = {}, []
    prompt = seq_len is not None
    if prompt:
        tps = seq_len // tm
        for dil in DILATED:
            out_specs.append(pl.BlockSpec((1, dil, tm // dil, 3 * BRANCH), lambda i: (i // tps, 0, i % tps, 0)))
            out_shape.append(jax.ShapeDtypeStruct((t // seq_len, dil, seq_len // dil, 3 * BRANCH), BF16))
        k_stack, v_stack, layer, stack_shape = keep
        skip = tps - stack_shape[2] // tm
        for stack in (k_stack, v_stack):
            if stack is None:
                out_specs.append(pl.BlockSpec((stack_shape[0], None, tm, BRANCH),
                                              lambda i: (0, i // tps, jnp.maximum(i % tps - skip, 0), 0)))
            else:
                aliases[len(args)] = len(out_shape)
                args.append(stack)
                in_specs.append(pl.BlockSpec(memory_space=pl.ANY))
                out_specs.append(pl.BlockSpec((None, None, tm, BRANCH),
                                              lambda i: (layer, i // tps, jnp.maximum(i % tps - skip, 0), 0)))
            out_shape.append(jax.ShapeDtypeStruct(stack_shape, F32))
        scratch = [pltpu.VMEM((3 * BRANCH // LANES, tm, LANES), F32)]
    return pl.pallas_call(
        functools.partial(_inproj_body, tm=tm, prompt=prompt, n_stack_in=len(aliases),
                          layer=keep[2] if prompt else 0),
        grid=(t // tm,),
        in_specs=in_specs,
        out_specs=out_specs,
        out_shape=out_shape,
        input_output_aliases=aliases,
        scratch_shapes=scratch,
        compiler_params=pltpu.CompilerParams(dimension_semantics=("arbitrary",), vmem_limit_bytes=VMEM_LIMIT),
        name="inproj",
    )(*args)


def _swa_body(q_ref, kp_ref, kc_ref, vp_ref, vc_ref, o_ref, lse_ref, *, dil, tq, n_sub):
    n = pl.program_id(1)
    q = (q_ref[...].astype(F32) * (HEAD_DIM ** -0.5)).astype(BF16)
    k = jnp.concatenate([kp_ref[...], kc_ref[...]], axis=0).astype(BF16)
    v = jnp.concatenate([vp_ref[...], vc_ref[...]], axis=0).astype(BF16)
    iq = lax.broadcasted_iota(jnp.int32, (tq, 2 * tq), 0)
    jk = lax.broadcasted_iota(jnp.int32, (tq, 2 * tq), 1)
    dist = iq + tq - jk
    in_band = (dist >= 0) & (dist <= ATTN_STEPS)
    bias = (dist * dil).astype(F32)
    head_bias = [ALIBI_SLOPES[h] * bias for h in range(N_HEADS)]
    first_valid = in_band & ((jk >= tq) | (n > 0))
    ones = jnp.ones((v.shape[0], HEAD_DIM), BF16)
    v_aug = [jnp.concatenate([v[:, h * HEAD_DIM:(h + 1) * HEAD_DIM], ones], axis=-1) for h in range(N_HEADS)]
    for i in range(n_sub):
        valid = first_valid if i == 0 else in_band
        rows = slice(i * tq, (i + 1) * tq)
        keys = slice(i * tq, (i + 2) * tq)
        outs, lses = [], []
        for h in range(N_HEADS):
            sl = slice(h * HEAD_DIM, (h + 1) * HEAD_DIM)
            s = lax.dot_general(q[rows, sl], k[keys, sl], (((1,), (1,)), ((), ())), preferred_element_type=F32)
            s = jnp.where(valid, s - head_bias[h], NEG_MASK)
            m = jnp.max(s, axis=-1, keepdims=True)
            p = jnp.exp(s - m)
            nd = jnp.dot(p.astype(BF16), v_aug[h][keys], preferred_element_type=F32)
            den = nd[:, HEAD_DIM:2 * HEAD_DIM]
            outs.append(nd[:, 0:HEAD_DIM] / den)
            lses.append(m + jnp.log(den))
        o_ref[rows, :] = jnp.concatenate(outs, axis=-1)
        lse_ref[rows, :] = jnp.concatenate(lses, axis=-1)


def _swa(qkv, dil, n_sub):
    b, _, n, _ = qkv.shape
    tq = ATTN_STEPS
    tqb = tq * n_sub

    def cur(col):
        return pl.BlockSpec((None, None, tqb, BRANCH), lambda zi, ni: (zi // dil, zi % dil, ni, col))

    def prev(col):
        return pl.BlockSpec((None, None, tq, BRANCH),
                            lambda zi, ni: (zi // dil, zi % dil, jnp.maximum(ni * n_sub - 1, 0), col))

    shp = jax.ShapeDtypeStruct((b, dil, n, BRANCH), F32)
    return pl.pallas_call(
        functools.partial(_swa_body, dil=dil, tq=tq, n_sub=n_sub),
        grid=(b * dil, n // tqb),
        in_specs=[cur(0), prev(1), cur(1), prev(2), cur(2)],
        out_specs=[cur(0), cur(0)],
        out_shape=[shp, shp],
        compiler_params=pltpu.CompilerParams(dimension_semantics=("arbitrary", "arbitrary"),
                                             vmem_limit_bytes=VMEM_LIMIT),
        name=f"swa_d{dil}",
    )(qkv, qkv, qkv, qkv, qkv)


def _merge_patterns(o_refs, lse_refs, scratch, tm):
    outs, lses = [], []
    si = 0
    for (_, dil), o_ref, lse_ref in zip(ATTN_PATTERNS, o_refs, lse_refs):
        if dil == 1:
            outs.append(o_ref[...])
            lses.append(lse_ref[...])
            continue
        ob, lb = scratch[si], scratch[si + 1]
        si += 2
        n_slab = BRANCH // LANES
        for r in range(dil):
            for c in range(n_slab):
                ob[c, pl.ds(r, tm // dil, stride=dil), :] = o_ref[r, :, c * LANES:(c + 1) * LANES]
                lb[c, pl.ds(r, tm // dil, stride=dil), :] = lse_ref[r, :, c * LANES:(c + 1) * LANES]
        outs.append(jnp.concatenate([ob[c] for c in range(n_slab)], axis=-1))
        lses.append(jnp.concatenate([lb[c] for c in range(n_slab)], axis=-1))
    l_all = lses[0]
    for l in lses[1:]:
        l_all = jnp.maximum(l_all, l)
    numer = jnp.zeros_like(l_all)
    denom = jnp.zeros_like(l_all)
    for o, l in zip(outs, lses):
        wgt = jnp.exp(l - l_all)
        numer = numer + wgt * o
        denom = denom + wgt
    return numer / denom


def _attn_sample_one(pa, kt, vt, t_new, buf):
    rows = N_HEADS * t_new
    q = pa[:, 0:BRANCH] * (HEAD_DIM ** -0.5)
    k_new = pa[:, BRANCH:2 * BRANCH]
    v_new = pa[:, 2 * BRANCH:3 * BRANCH]
    gate = pa[:, 3 * BRANCH:4 * BRANCH]
    row_h = lax.broadcasted_iota(jnp.int32, (rows, BRANCH), 0) // t_new
    lane_h = lax.broadcasted_iota(jnp.int32, (rows, BRANCH), 1) // HEAD_DIM
    own = row_h == lane_h
    q_bd = jnp.where(own, jnp.concatenate([q] * N_HEADS, axis=0), 0.0).astype(BF16)
    pad = jnp.zeros((LANES - t_new, BRANCH), F32)
    k_new_p = jnp.concatenate([k_new, pad], axis=0).astype(BF16)
    v_new_p = jnp.concatenate([v_new, pad], axis=0).astype(BF16)
    kt = kt.astype(BF16)
    vt = vt.astype(BF16)
    nt = (((1,), (1,)), ((), ()))
    s = jnp.concatenate([jnp.dot(q_bd, kt, preferred_element_type=F32),
                         lax.dot_general(q_bd, k_new_p, nt, preferred_element_type=F32)], axis=-1)
    ncol = buf + LANES
    col = lax.broadcasted_iota(jnp.int32, (rows, ncol), 1)
    row = lax.broadcasted_iota(jnp.int32, (rows, ncol), 0)
    delta = buf + row % t_new - col
    hrow = row // t_new
    slope = jnp.where(hrow == 0, ALIBI_SLOPES[0],
                      jnp.where(hrow == 1, ALIBI_SLOPES[1], jnp.where(hrow == 2, ALIBI_SLOPES[2], ALIBI_SLOPES[3])))
    s = s - slope * delta.astype(F32)
    ps, ms, dens = [], [], []
    for win, dil in ATTN_PATTERNS:
        valid = (delta >= 0) & (delta <= win) & ((delta & (dil - 1)) == 0)
        sp = jnp.where(valid, s, NEG_MASK)
        m = jnp.max(sp, axis=-1, keepdims=True)
        p = jnp.exp(sp - m)
        ps.append(p.astype(BF16))
        ms.append(m)
        dens.append(jnp.sum(p, axis=-1, keepdims=True))
    p_all = jnp.concatenate(ps, axis=0)
    num_all = (lax.dot_general(p_all[:, :buf], vt, nt, preferred_element_type=F32)
               + jnp.dot(p_all[:, buf:], v_new_p, preferred_element_type=F32))
    m_all = jnp.maximum(jnp.maximum(ms[0], ms[1]), ms[2])
    numer = jnp.zeros((rows, BRANCH), F32)
    denom = jnp.zeros((rows, 1), F32)
    for g in range(len(ATTN_PATTERNS)):
        wgt = jnp.exp(ms[g] - m_all)
        numer = numer + wgt * num_all[g * rows:(g + 1) * rows]
        denom = denom + wgt * dens[g]
    o_full = jnp.where(own, numer / denom, 0.0)
    o = o_full[0:t_new]
    for h in range(1, N_HEADS):
        o = o + o_full[h * t_new:(h + 1) * t_new]
    return o * _silu(gate)


def _attn_sample_body(pa_ref, kt_ref, vt_ref, out_ref, *, t_new, buf, nb):
    for j in range(nb):
        out_ref[j] = _attn_sample_one(pa_ref[j], kt_ref[j], vt_ref[j], t_new, buf)


def _store_state(s_out_ref, state, create_layer):
    if create_layer is None:
        s_out_ref[...] = state
    else:
        s_out_ref[...] = jnp.zeros(s_out_ref.shape, F32)
        s_out_ref[create_layer] = state


def _delta_body(*refs, ns, tl, chunk, n_tiles, create_layer):
    pb_ref, pe_ref, cpre_ref, s0_ref, cw_ref, alog_ref, dtb_ref, nw_ref = refs[0:8]
    y_ref, s_out_ref, xbuf, s_scr = refs[-4:]
    t = pl.program_id(1)
    nc = tl // chunk
    nb = ns * nc
    nbh = nb * N_HEADS
    rows = ns * tl
    width = 3 * BRANCH

    @pl.when(t == 0)
    def _():
        xbuf[:, 0:SUBLANES, :] = cpre_ref[...]
        s_scr[...] = s0_ref[...]

    xbuf[:, SUBLANES:SUBLANES + tl, :] = pb_ref[:, :, 0:width]
    conv = jnp.zeros((ns, tl, width), F32)
    for tap in range(CONV_TAPS):
        off = SUBLANES - (CONV_TAPS - 1) + tap
        conv = conv + xbuf[:, off:off + tl, :] * cw_ref[tap:tap + 1, :]
    if n_tiles > 1:
        xbuf[:, 0:SUBLANES, :] = xbuf[:, tl:tl + SUBLANES, :]
    conv = _silu(conv).reshape(rows, width)
    g_ones = _head_ones()
    q = conv[:, 0:BRANCH]
    k = conv[:, BRANCH:2 * BRANCH]
    v = conv[:, 2 * BRANCH:3 * BRANCH]
    q = q * lax.rsqrt(_head_sum(q * q, g_ones) + 1e-6) * (HEAD_DIM ** -0.5)
    k = k * lax.rsqrt(_head_sum(k * k, g_ones) + 1e-6)
    pe = pe_ref[...].reshape(rows, LANES)
    lane = lax.broadcasted_iota(jnp.int32, (rows, LANES), 1)
    gb = jnp.where(lane < N_HEADS, _sigmoid(pe), -jnp.exp(alog_ref[...]) * _softplus(pe + dtb_ref[...]))

    qh = _to_head_batch(q, nb, chunk)
    kh = _to_head_batch(k, nb, chunk)
    vh = _to_head_batch(v, nb, chunk)

    def col_batch(first):
        cols = [gb[:, first + h:first + h + 1].reshape(nb, 1, chunk, 1) for h in range(N_HEADS)]
        return jnp.concatenate(cols, axis=1).reshape(nbh, chunk, 1)

    beta = col_batch(0)
    g = col_batch(N_HEADS)
    ri = lax.broadcasted_iota(jnp.int32, (nbh, chunk, chunk), 1)
    ci = lax.broadcasted_iota(jnp.int32, (nbh, chunk, chunk), 2)
    ltri = jnp.where(ri >= ci, 1.0, 0.0).astype(BF16)
    eye = jnp.where(ri == ci, 1.0, 0.0).astype(F32)
    gm = jnp.where(ri > ci, jnp.broadcast_to(g, (nbh, chunk, chunk)), 0.0)
    dmat = sum(_bmm(ltri, term) for term in _split3(gm))
    gc = dmat[:, :, 0:1] + g[:, 0:1, :]
    g_last = gc[:, chunk - 1:chunk, :]
    decay = jnp.where(ri >= ci, jnp.exp(dmat), 0.0)
    kb = kh * beta
    vb = vh * beta
    kh_b = kh.astype(BF16)
    a_mat = jnp.where(ri > ci, _bmm_nt(kb.astype(BF16), kh_b) * decay, 0.0)
    n_fac = int(math.log2(chunk))
    x = -a_mat
    u = eye + x
    x = _bmm1(x, x)
    for _ in range(2, n_fac):
        prod = _bmm1(jnp.concatenate([x, u], axis=1), x)
        x = prod[:, 0:chunk]
        u = u + prod[:, chunk:2 * chunk]
    t_inv = u + _bmm1(u, x)
    uw = _bmm3(t_inv, jnp.concatenate([vb, kb * jnp.exp(gc)], axis=-1))
    u_mat = uw[:, :, 0:HEAD_DIM]
    w_mat = uw[:, :, HEAD_DIM:2 * HEAD_DIM]
    qk = _bmm_nt(qh.astype(BF16), kh_b) * decay
    qg = qh * jnp.exp(gc)
    kg = kh * jnp.exp(g_last - gc)
    e_last = jnp.exp(g_last)

    def sel(a, c):
        a5 = a.reshape((ns, nc, N_HEADS) + a.shape[1:])
        return a5[:, c].reshape((ns * N_HEADS,) + a.shape[1:])

    st = s_scr[...].reshape(ns * N_HEADS, HEAD_DIM, HEAD_DIM)
    o_chunks = []
    for c in range(nc):
        s_b = st.astype(BF16)
        wq = _bmm(jnp.concatenate([sel(w_mat, c), sel(qg, c)], axis=1).astype(BF16), s_b)
        v_new = (sel(u_mat, c) - wq[:, 0:chunk]).astype(BF16)
        o_chunks.append(wq[:, chunk:2 * chunk] + _bmm(sel(qk, c).astype(BF16), v_new))
        st = st * sel(e_last, c) + _bmm_tn(sel(kg, c).astype(BF16), v_new)
    s_scr[...] = st.reshape(ns, N_HEADS, HEAD_DIM, HEAD_DIM)
    if nc > 1:
        o_all = jnp.concatenate([oc.reshape(ns, 1, N_HEADS, chunk, HEAD_DIM) for oc in o_chunks], axis=1)
        o_all = o_all.reshape(nbh, chunk, HEAD_DIM)
    else:
        o_all = o_chunks[0]
    o = _from_head_batch(o_all, nb, chunk)
    o = o * lax.rsqrt(_head_sum(o * o, g_ones) * (1.0 / HEAD_DIM) + NORM_EPS) * nw_ref[...]
    gate = pb_ref[:, :, width:width + BRANCH].reshape(rows, BRANCH)
    y_ref[...] = (o * _silu(gate)).reshape(ns, tl, BRANCH)

    @pl.when(t == n_tiles - 1)
    def _():
        _store_state(s_out_ref, s_scr[...], create_layer)


def _state_spec(ns, layer):
    return pl.BlockSpec((None, ns, N_HEADS, HEAD_DIM, HEAD_DIM), lambda bi, ti: (layer, bi, 0, 0, 0))


def _state_out(ns, layer, s_stack, depth):
    if s_stack is not None:
        return _state_spec(ns, layer), [s_stack], None
    spec = pl.BlockSpec((depth, ns, N_HEADS, HEAD_DIM, HEAD_DIM), lambda bi, ti: (0, bi, 0, 0, 0))
    return spec, [], layer


def _delta(pb, pe, cpre, s0, s0_layer, s_stack, layer, depth, cw, alog_row, dtb_row, nw_row, *, ns, tl, chunk):
    b, l, _ = pb.shape
    n_tiles = l // tl
    seq3 = lambda w: pl.BlockSpec((ns, tl, w), lambda bi, ti: (bi, ti, 0))
    const2 = lambda a: pl.BlockSpec(a.shape, lambda bi, ti: (0, 0))
    out_state, alias_args, create_layer = _state_out(ns, layer, s_stack, depth)
    return pl.pallas_call(
        functools.partial(_delta_body, ns=ns, tl=tl, chunk=chunk, n_tiles=n_tiles, create_layer=create_layer),
        grid=(b // ns, n_tiles),
        in_specs=[seq3(4 * BRANCH), seq3(LANES),
                  pl.BlockSpec((ns, SUBLANES, 3 * BRANCH), lambda bi, ti: (bi, 0, 0)), _state_spec(ns, s0_layer),
                  const2(cw), const2(alog_row), const2(dtb_row), const2(nw_row)]
        + [pl.BlockSpec(memory_space=pl.ANY)] * len(alias_args),
        out_specs=[seq3(BRANCH), out_state],
        out_shape=[jax.ShapeDtypeStruct((b, l, BRANCH), F32),
                   jax.ShapeDtypeStruct((depth, b, N_HEADS, HEAD_DIM, HEAD_DIM), F32)],
        input_output_aliases={8: 1} if alias_args else {},
        scratch_shapes=[pltpu.VMEM((ns, SUBLANES + tl, 3 * BRANCH), F32),
                        pltpu.VMEM((ns, N_HEADS, HEAD_DIM, HEAD_DIM), F32)],
        compiler_params=pltpu.CompilerParams(dimension_semantics=("arbitrary", "arbitrary"),
                                             vmem_limit_bytes=VMEM_LIMIT),
        name="delta",
    )(pb, pe, cpre, s0, cw, alog_row, dtb_row, nw_row, *alias_args)


def _hgrn_body(*refs, ns, tl, chunk, n_tiles, layer, create_layer):
    pc_ref, lbraw_ref, s0_ref, nw_ref = refs[0:4]
    y_ref, s_out_ref, s_scr = refs[-3:]
    t = pl.program_id(1)
    nc = tl // chunk
    nb = ns * nc
    rows = ns * tl

    def flip(s4):
        return jnp.swapaxes(s4.reshape(ns * N_HEADS, HEAD_DIM, HEAD_DIM), 1, 2).reshape(s4.shape)

    @pl.when(t == 0)
    def _():
        s_scr[...] = flip(s0_ref[...])

    raw = lbraw_ref[...]
    e = jnp.exp(raw - jnp.max(raw, axis=0, keepdims=True))
    sm = e / jnp.sum(e, axis=0, keepdims=True)
    lb = jnp.zeros((1, BRANCH), F32)
    for d in range(1, layer + 1):
        lb = lb + sm[d:d + 1, :]

    pc = pc_ref[...].reshape(rows, 4 * BRANCH)
    qh = _silu(pc[:, 0:BRANCH])
    fr = pc[:, BRANCH:2 * BRANCH]
    vh = pc[:, 2 * BRANCH:3 * BRANCH]
    gate = pc[:, 3 * BRANCH:4 * BRANCH]
    f = lb + (1.0 - lb) * _sigmoid(fr)
    log_f = jnp.log(f).reshape(nb, chunk, BRANCH)
    kh = (1.0 - lb) * _sigmoid(-fr)

    q3 = qh.reshape(nb, chunk, BRANCH)
    k3 = kh.reshape(nb, chunk, BRANCH)
    v3 = vh.reshape(nb, chunk, BRANCH)
    ri = lax.broadcasted_iota(jnp.int32, (nb, chunk, chunk), 1)
    ci = lax.broadcasted_iota(jnp.int32, (nb, chunk, chunk), 2)
    ltri = jnp.where(ri >= ci, 1.0, 0.0).astype(BF16)
    l1 = log_f.astype(BF16)
    r1 = log_f - l1.astype(F32)
    l2 = r1.astype(BF16)
    l3 = (r1 - l2.astype(F32)).astype(BF16)
    gcum = _bmm(ltri, l1) + _bmm(ltri, l2) + _bmm(ltri, l3)
    g_last = gcum[:, chunk - 1:chunk, :]
    qg = q3 * jnp.exp(gcum)
    kg = k3 * jnp.exp(g_last - gcum)
    e_last = jnp.exp(g_last)

    g_ones = _head_ones()
    n_grp = chunk // SUBLANES
    row_i = [g * SUBLANES + lax.broadcasted_iota(jnp.int32, (nb, chunk - g * SUBLANES, BRANCH), 1)
             for g in range(n_grp)]
    acc = [jnp.zeros((nb, SUBLANES, BRANCH), F32) for _ in range(n_grp)]
    for j in range(chunk):
        g0 = j // SUBLANES
        lo = g0 * SUBLANES
        dec = jnp.exp(jnp.where(row_i[g0] >= j, gcum[:, lo:, :] - gcum[:, j:j + 1, :], NEG_MASK))
        t_j = (q3[:, lo:, :] * dec * k3[:, j:j + 1, :]).reshape(nb * (chunk - lo), BRANCH)
        a_j = jnp.dot(t_j.astype(BF16), g_ones, preferred_element_type=F32).reshape(nb, chunk - lo, BRANCH)
        c_j = a_j * v3[:, j:j + 1, :]
        for gi in range(g0, n_grp):
            acc[gi] = acc[gi] + c_j[:, (gi - g0) * SUBLANES:(gi - g0 + 1) * SUBLANES, :]
    intra = jnp.concatenate(acc, axis=1) if n_grp > 1 else acc[0]

    qg_b = _to_head_batch(qg.reshape(rows, BRANCH), nb, chunk).astype(BF16)
    kg_b = _to_head_batch(kg.reshape(rows, BRANCH), nb, chunk).astype(BF16)
    v_b = _to_head_batch(vh, nb, chunk).astype(BF16)
    el_b = _to_head_batch(e_last.reshape(nb, BRANCH), nb, 1)

    def sel(a, c):
        a5 = a.reshape((ns, nc, N_HEADS) + a.shape[1:])
        return a5[:, c].reshape((ns * N_HEADS,) + a.shape[1:])

    st = s_scr[...].reshape(ns * N_HEADS, HEAD_DIM, HEAD_DIM)
    o_chunks = []
    for c in range(nc):
        o_chunks.append(_bmm_nt(sel(qg_b, c), st.astype(BF16)))
        st = st * sel(el_b, c) + _bmm_tn(sel(v_b, c), sel(kg_b, c))
    s_scr[...] = st.reshape(ns, N_HEADS, HEAD_DIM, HEAD_DIM)
    if nc > 1:
        o_all = jnp.concatenate([oc.reshape(ns, 1, N_HEADS, chunk, HEAD_DIM) for oc in o_chunks], axis=1)
        o_all = o_all.reshape(nb * N_HEADS, chunk, HEAD_DIM)
    else:
        o_all = o_chunks[0]
    o = _from_head_batch(o_all, nb, chunk) + intra.reshape(rows, BRANCH)
    o = o * lax.rsqrt(_head_sum(o * o, g_ones) * (1.0 / HEAD_DIM) + NORM_EPS) * nw_ref[...]
    y_ref[...] = (o * _silu(gate)).reshape(ns, tl, BRANCH)

    @pl.when(t == n_tiles - 1)
    def _():
        _store_state(s_out_ref, flip(s_scr[...]), create_layer)


def _hgrn_attn_body(*refs, hgrn_args, attn_args):
    pa_ref, kt_ref, vt_ref, y_ref, s_out_ref, ya_ref, s_scr = refs[-7:]
    _hgrn_body(*refs[:-7], y_ref, s_out_ref, s_scr, **hgrn_args)
    _attn_sample_body(pa_ref, kt_ref, vt_ref, ya_ref, **attn_args)


def _hgrn(pc, lb_raw, s0, s0_layer, s_stack, depth, nw_row, *, ns, tl, chunk, layer, rider=None):
    b, l, _ = pc.shape
    n_tiles = l // tl
    out_state, alias_args, create_layer = _state_out(ns, layer, s_stack, depth)
    hgrn_args = dict(ns=ns, tl=tl, chunk=chunk, n_tiles=n_tiles, layer=layer, create_layer=create_layer)
    in_specs = [pl.BlockSpec((ns, tl, 4 * BRANCH), lambda bi, ti: (bi, ti, 0)),
                pl.BlockSpec(lb_raw.shape, lambda bi, ti: (0, 0)), _state_spec(ns, s0_layer),
                pl.BlockSpec(nw_row.shape, lambda bi, ti: (0, 0))] + [pl.BlockSpec(memory_space=pl.ANY)] * len(alias_args)
    out_specs = [pl.BlockSpec((ns, tl, BRANCH), lambda bi, ti: (bi, ti, 0)), out_state]
    out_shape = [jax.ShapeDtypeStruct((b, l, BRANCH), F32),
                 jax.ShapeDtypeStruct((depth, b, N_HEADS, HEAD_DIM, HEAD_DIM), F32)]
    args = [pc, lb_raw, s0, nw_row, *alias_args]
    body = functools.partial(_hgrn_body, **hgrn_args)
    if rider is not None:
        pa_s, cache_kt, cache_vt = rider
        assert b == ns and pa_s.shape[0] % n_tiles == 0
        nb = pa_s.shape[0] // n_tiles
        t_new, buf = pa_s.shape[1], cache_kt.shape[3]
        cache = pl.BlockSpec((None, nb, BRANCH, buf), lambda bi, ti: (layer, ti, 0, 0))
        in_specs += [pl.BlockSpec((nb, t_new, 4 * BRANCH), lambda bi, ti: (ti, 0, 0)), cache, cache]
        out_specs.append(pl.BlockSpec((nb, t_new, BRANCH), lambda bi, ti: (ti, 0, 0)))
        out_shape.append(jax.ShapeDtypeStruct((pa_s.shape[0], t_new, BRANCH), F32))
        args += [pa_s, cache_kt, cache_vt]
        body = functools.partial(_hgrn_attn_body, hgrn_args=hgrn_args, attn_args=dict(t_new=t_new, buf=buf, nb=nb))
    return pl.pallas_call(
        body,
        grid=(b // ns, n_tiles),
        in_specs=in_specs,
        out_specs=out_specs,
        out_shape=out_shape,
        input_output_aliases={4: 1} if alias_args else {},
        scratch_shapes=[pltpu.VMEM((ns, N_HEADS, HEAD_DIM, HEAD_DIM), F32)],
        compiler_params=pltpu.CompilerParams(dimension_semantics=("arbitrary", "arbitrary"),
                                             vmem_limit_bytes=VMEM_LIMIT),
        name="hgrn" if rider is None else "hgrn_attn",
    )(*args)


def _post_body(*refs, ns, tl, n_tiles, start_pos, final, n_attn):
    attn_refs = refs[0:n_attn]
    (x_ref, yb_ref, yc_ref, pd_ref, ppre_ref, pw_ref, ps_ref, wo_ref, fw_ref, out_ref, xbuf) = refs[n_attn:n_attn + 11]
    slabs = refs[n_attn + 11:]
    t = pl.program_id(1)
    rows = ns * tl
    if n_attn == 1:
        ya = attn_refs[0][...].reshape(rows, BRANCH)
    else:
        n_g = len(ATTN_PATTERNS)
        ya = _merge_patterns(attn_refs[0:n_g], attn_refs[n_g:2 * n_g], slabs, tl) * _silu(attn_refs[2 * n_g][...])

    @pl.when(t == 0)
    def _():
        xbuf[:, 0:POOL_MAX, :] = ppre_ref[...]

    xbuf[:, POOL_MAX:POOL_MAX + tl, :] = pd_ref[:, :, 0:BRANCH]

    def back(kk):
        return xbuf[:, POOL_MAX - kk:POOL_MAX - kk + tl, :]

    x0 = back(0)
    sums = {}
    acc = x0
    for kk in range(1, POOL_MAX):
        acc = acc + back(kk)
        if kk + 1 in POOL_WINDOWS:
            sums[kk + 1] = acc
    if n_tiles > 1:
        xbuf[:, 0:POOL_MAX, :] = xbuf[:, tl:tl + POOL_MAX, :]
    group = lax.broadcasted_iota(jnp.int32, (ns, tl, BRANCH), 2) // HEAD_DIM
    pos = start_pos + t * tl + lax.broadcasted_iota(jnp.int32, (ns, tl, BRANCH), 1)
    tot = sums[POOL_WINDOWS[-1]]
    win = jnp.full((ns, tl, BRANCH), POOL_WINDOWS[-1], jnp.int32)
    for gi in range(len(POOL_WINDOWS) - 2, -1, -1):
        tot = jnp.where(group == gi, sums[POOL_WINDOWS[gi]], tot)
        win = jnp.where(group == gi, POOL_WINDOWS[gi], win)
    cnt = jnp.minimum(pos + 1, win).astype(F32)
    pooled = (tot / cnt - x0).reshape(rows, BRANCH)
    gate_d = pd_ref[:, :, BRANCH:2 * BRANCH].reshape(rows, BRANCH)
    yd = jnp.dot(pooled.astype(BF16), pw_ref[...], preferred_element_type=F32) * ps_ref[...] * _silu(gate_d)
    ycat = jnp.concatenate([ya, yb_ref[...].reshape(rows, BRANCH),
                            yc_ref[...].reshape(rows, BRANCH), yd], axis=-1).astype(BF16)
    x_new = x_ref[...].reshape(rows, -1) + jnp.dot(ycat, wo_ref[...], preferred_element_type=F32)
    if final:
        ms = jnp.mean(x_new * x_new, axis=-1, keepdims=True)
        x_new = x_new * lax.rsqrt(ms + NORM_EPS) * fw_ref[...]
    out_ref[...] = x_new.reshape(out_ref.shape)


def _post(attn, x, yb, yc, pd, ppre, pw_bd, ps_row, wo, fw_row, *, ns, tl, start_pos, final):
    b, l, d = x.shape
    n_tiles = l // tl
    seq3 = lambda w: pl.BlockSpec((ns, tl, w), lambda bi, ti: (bi, ti, 0))
    const2 = lambda a: pl.BlockSpec(a.shape, lambda bi, ti: (0, 0))
    if len(attn) == 1:
        attn_specs, slabs = [seq3(BRANCH)], []
    else:
        assert ns == 1

        def cls(dil):
            if dil == 1:
                return pl.BlockSpec((None, None, tl, BRANCH), lambda bi, ti: (bi, 0, ti, 0))
            return pl.BlockSpec((None, dil, tl // dil, BRANCH), lambda bi, ti: (bi, 0, ti, 0))

        attn_specs = 2 * [cls(dil) for _, dil in ATTN_PATTERNS] + [
            pl.BlockSpec((None, tl, BRANCH), lambda bi, ti: (bi, ti, 3))]
        slabs = [pltpu.VMEM((BRANCH // LANES, tl, LANES), F32)] * (2 * len(DILATED))
    return pl.pallas_call(
        functools.partial(_post_body, ns=ns, tl=tl, n_tiles=n_tiles, start_pos=start_pos, final=final,
                          n_attn=len(attn)),
        grid=(b // ns, n_tiles),
        in_specs=attn_specs + [seq3(d), seq3(BRANCH), seq3(BRANCH), seq3(2 * BRANCH),
                               pl.BlockSpec((ns, POOL_MAX, BRANCH), lambda bi, ti: (bi, 0, 0)),
                               const2(pw_bd), const2(ps_row), const2(wo), const2(fw_row)],
        out_specs=seq3(d),
        out_shape=jax.ShapeDtypeStruct((b, l, d), F32),
        scratch_shapes=[pltpu.VMEM((ns, POOL_MAX + tl, BRANCH), F32)] + slabs,
        compiler_params=pltpu.CompilerParams(dimension_semantics=("arbitrary", "arbitrary"),
                                             vmem_limit_bytes=VMEM_LIMIT),
        name="post",
    )(*attn, x, yb, yc, pd, ppre, pw_bd, ps_row, wo, fw_row)


def _prep_weights(w_in, w_out, pool_w):
    n_small = 2 * N_HEADS
    main = 8 * BRANCH
    w_perm = (w_in[:, :, 0:main].astype(BF16), w_in[:, :, main + n_small:].astype(BF16),
              jnp.pad(w_in[:, :, main:main + n_small], ((0, 0), (0, 0), (0, LANES - n_small))).astype(BF16))
    depth, groups = pool_w.shape[0:2]
    pw_bd = jnp.zeros((depth, BRANCH, BRANCH), pool_w.dtype)
    for g in range(groups):
        sl = slice(g * HEAD_DIM, (g + 1) * HEAD_DIM)
        pw_bd = pw_bd.at[:, sl, sl].set(pool_w[:, g])
    return w_perm, w_out.astype(BF16), pw_bd.astype(BF16)


def _attn_prompt(pa3, classes, max_sub):
    b, l, w = pa3.shape
    os_, lses = [], []
    ci = 0
    for _, dil in ATTN_PATTERNS:
        n_sub = min(max_sub, l // dil // ATTN_STEPS)
        if dil == 1:
            o, lse = _swa(pa3.reshape(b, 1, l, w), 1, n_sub)
        else:
            o, lse = _swa(classes[ci], dil, n_sub)
            ci += 1
        os_.append(o)
        lses.append(lse)
    return os_ + lses + [pa3]


def _trunk(x, start_pos, states, prm, cfg):
    b, l, d = x.shape
    depth = prm['norm_w'].shape[0]
    dt = x.dtype
    ns, tl = cfg['ns'], cfg['tl']
    prompt = states is None
    state_shape = (depth, b, N_HEADS, HEAD_DIM, HEAD_DIM)
    d_stack = h_stack = k_stack = v_stack = None
    if prompt:
        keep = min(ATTN_PATTERNS[-1][0], l)
        kv_shape = (depth, b, keep, BRANCH)
        d_in = h_in = jnp.zeros((1,) + state_shape[1:], F32)
    else:
        c_kt, c_vt, d_in, c_conv, h_in, c_pool = states
    ks, vs, dcs, ps = [], [], [], []
    for layer in range(depth):
        if prompt:
            conv_prefix = jnp.zeros((b, CONV_TAPS - 1, 3 * BRANCH), dt)
            pool_prefix = jnp.zeros((b, POOL_MAX - 1, BRANCH), dt)
        else:
            conv_prefix, pool_prefix = c_conv[layer], c_pool[layer]
        outs = _inproj(x.reshape(b * l, d), prm['norm_w'][layer][None, :], tuple(wi[layer] for wi in prm['w_in']), cfg['tm'],
                       seq_len=l if prompt else None, keep=(k_stack, v_stack, layer, kv_shape) if prompt else None)
        pa, pb, pc, pd, pe = outs[0:5]
        pa3 = pa.reshape(b, l, 4 * BRANCH)
        s0_layer = 0 if prompt else layer
        hgrn_call = functools.partial(
            _hgrn, pc.reshape(b, l, 4 * BRANCH), prm['hgrn_lb_raw'], h_in, s0_layer, h_stack, depth,
            prm['hgrn_nw_row'][layer], ns=ns, tl=cfg['tl_hgrn'], chunk=math.gcd(l, HGRN_CHUNK), layer=layer)
        if prompt:
            n_cls = len(DILATED)
            attn = _attn_prompt(pa3, outs[5:5 + n_cls], cfg['swa_sub'])
            k_stack, v_stack = outs[5 + n_cls:]
            yc, h_stack = yield hgrn_call
        else:
            attn = [(yield pa3)]
            ks.append(pa3[:, :, BRANCH:2 * BRANCH].reshape(b, l, N_HEADS, HEAD_DIM))
            vs.append(pa3[:, :, 2 * BRANCH:3 * BRANCH].reshape(b, l, N_HEADS, HEAD_DIM))
            yc, h_stack = hgrn_call()

        pb3 = pb.reshape(b, l, 4 * BRANCH)
        cpre = jnp.pad(conv_prefix, ((0, 0), (SUBLANES - (CONV_TAPS - 1), 0), (0, 0)))
        yb, d_stack = _delta(pb3, pe.reshape(b, l, LANES), cpre, d_in, s0_layer, d_stack, layer, depth,
                             prm['delta_conv_w'][layer], prm['alog_row'][layer], prm['dtb_row'][layer],
                             prm['delta_nw_row'][layer], ns=ns, tl=tl, chunk=math.gcd(l, DELTA_CHUNK))
        dcs.append(jnp.concatenate([conv_prefix, pb3[:, :, 0:3 * BRANCH]], axis=1)[:, -(CONV_TAPS - 1):])

        pd3 = pd.reshape(b, l, 2 * BRANCH)
        ppre = jnp.pad(pool_prefix, ((0, 0), (1, 0), (0, 0)))
        ps.append(jnp.concatenate([pool_prefix, pd3[:, :, 0:BRANCH]], axis=1)[:, -(POOL_MAX - 1):])
        x = _post(attn, x, yb, yc, pd3, ppre, prm['pool_w_bd'][layer], prm['pool_scale'][layer][None, :],
                  prm['w_out'][layer], prm['final_norm_w'][None, :], ns=cfg['ns_post'], tl=cfg['tl_post'],
                  start_pos=start_pos, final=(layer == depth - 1))
    if prompt:
        k_all = k_stack.reshape(depth, b, keep, N_HEADS, HEAD_DIM)
        v_all = v_stack.reshape(depth, b, keep, N_HEADS, HEAD_DIM)
    else:
        k_all, v_all = jnp.stack(ks), jnp.stack(vs)
    return x, (k_all, v_all, d_stack, jnp.stack(dcs), h_stack, jnp.stack(ps))


def _lane_row(vals, offset):
    depth, n = vals.shape
    return jnp.zeros((depth, 1, LANES), F32).at[:, 0, offset:offset + n].set(vals.astype(F32))


def kernel(x_prompt, x_sample, cache_attn_k, cache_attn_v, state_delta, state_delta_conv, state_hgrn, state_pool,
           norm_w, w_in, w_out, delta_conv_w, delta_a_log, delta_dt_bias, delta_norm_w, hgrn_lb_raw, hgrn_norm_w,
           pool_w, pool_scale, final_norm_w):
    w_in_p, w_out_b, pw_bd = _prep_weights(w_in, w_out, pool_w)
    prm = dict(
        norm_w=norm_w, w_in=w_in_p, w_out=w_out_b, pool_w_bd=pw_bd, pool_scale=pool_scale,
        final_norm_w=final_norm_w, delta_conv_w=delta_conv_w, hgrn_lb_raw=hgrn_lb_raw,
        alog_row=_lane_row(delta_a_log, N_HEADS), dtb_row=_lane_row(delta_dt_bias, N_HEADS),
        delta_nw_row=jnp.tile(delta_norm_w, (1, N_HEADS))[:, None, :],
        hgrn_nw_row=jnp.tile(hgrn_norm_w, (1, N_HEADS))[:, None, :],
    )
    bp = x_prompt.shape[0]
    dec_b, dec_l = x_sample.shape[0], x_sample.shape[1]
    depth, _, buf = cache_attn_k.shape[0:3]
    ckt = jnp.transpose(cache_attn_k, (0, 1, 3, 4, 2)).reshape(depth, dec_b, BRANCH, buf)
    cvt = jnp.transpose(cache_attn_v, (0, 1, 3, 4, 2)).reshape(depth, dec_b, BRANCH, buf)
    run_p = _trunk(x_prompt, 0, None, prm,
                   dict(tm=512, ns=bp, tl=256, tl_hgrn=128, ns_post=1, tl_post=1024, swa_sub=4))
    run_s = _trunk(x_sample, PAST_LEN, (ckt, cvt, state_delta, state_delta_conv, state_hgrn, state_pool), prm,
                   dict(tm=256, ns=32, tl=dec_l, tl_hgrn=dec_l, ns_post=32, tl_post=dec_l))
    hgrn_call, pa_s = next(run_p), next(run_s)
    for _ in range(depth):
        yc, h_stack, ya_s = hgrn_call(rider=(pa_s, ckt, cvt))
        try:
            hgrn_call = run_p.send((yc, h_stack))
        except StopIteration as done:
            y_p, st_p = done.value
        try:
            pa_s = run_s.send(ya_s)
        except StopIteration as done:
            y_s, st_s = done.value
    k_p, v_p, d_p, dc_p, h_p, pl_p = st_p
    k_s, v_s, d_s, dc_s, h_s, pl_s = st_s
    return (y_p, y_s, k_p, k_s, v_p, v_s, d_p, d_s, dc_p, dc_s, h_p, h_s, pl_p, pl_s)
```

```python
import functools
import math

import jax
import jax.numpy as jnp
from jax import lax
from jax.experimental import pallas as pl
from jax.experimental.pallas import tpu as pltpu

F32 = jnp.float32
BF16 = jnp.bfloat16

N_HEADS = 4
HEAD_DIM = 64
BRANCH = N_HEADS * HEAD_DIM
ATTN_PATTERNS = ((128, 1), (512, 4), (2048, 16))
ATTN_STEPS = 128
CONV_TAPS = 4
DELTA_CHUNK = 64
HGRN_CHUNK = 16
POOL_WINDOWS = (2, 4, 8, 16)
POOL_MAX = 16
PAST_LEN = 2048
NORM_EPS = 1e-6
L2_EPS = 1e-6
NEG_MASK = -1e30
ALIBI_SLOPES = tuple(2.0 ** (-8.0 * (h + 1) / N_HEADS) for h in range(N_HEADS))

SUBLANES = 8
LANES = 128
VMEM_LIMIT = 56 * 1024 * 1024

P_WIDTHS = (4 * BRANCH, 4 * BRANCH, 4 * BRANCH, 2 * BRANCH, LANES)
DILATED = tuple(d for _, d in ATTN_PATTERNS if d > 1)


def _sigmoid(x):
    return 1.0 / (1.0 + jnp.exp(-x))


def _silu(x):
    return x * _sigmoid(x)


def _softplus(x):
    return jnp.maximum(x, 0.0) + jnp.log(1.0 + jnp.exp(-jnp.abs(x)))


def _head_ones():
    r = lax.broadcasted_iota(jnp.int32, (BRANCH, BRANCH), 0) // HEAD_DIM
    c = lax.broadcasted_iota(jnp.int32, (BRANCH, BRANCH), 1) // HEAD_DIM
    return jnp.where(r == c, 1.0, 0.0).astype(BF16)


def _split2(x):
    hi = x.astype(BF16)
    return hi, (x - hi.astype(F32)).astype(BF16)


def _split3(x):
    t1 = x.astype(BF16)
    r1 = x - t1.astype(F32)
    t2 = r1.astype(BF16)
    return t1, t2, (r1 - t2.astype(F32)).astype(BF16)


def _head_sum(x, g):
    hi, lo = _split2(x)
    return jnp.dot(hi, g, preferred_element_type=F32) + jnp.dot(lo, g, preferred_element_type=F32)


def _bmm(a, b):
    return jnp.einsum('bij,bjk->bik', a, b, preferred_element_type=F32)


def _bmm_nt(a, b):
    return jnp.einsum('bid,bjd->bij', a, b, preferred_element_type=F32)


def _bmm_tn(a, b):
    return jnp.einsum('bci,bcj->bij', a, b, preferred_element_type=F32)


def _bmm1(a, b):
    return _bmm(a.astype(BF16), b.astype(BF16))


def _bmm3(a, b):
    ah, al = _split2(a)
    bh, bl = _split2(b)
    return _bmm(ah, bh) + _bmm(ah, bl) + _bmm(al, bh)


def _to_head_batch(x2d, nb, chunk):
    parts = [x2d[:, h * HEAD_DIM:(h + 1) * HEAD_DIM].reshape(nb, 1, chunk, HEAD_DIM) for h in range(N_HEADS)]
    return jnp.concatenate(parts, axis=1).reshape(nb * N_HEADS, chunk, HEAD_DIM)


def _from_head_batch(x, nb, chunk):
    x4 = x.reshape(nb, N_HEADS, chunk, HEAD_DIM)
    return jnp.concatenate([x4[:, h].reshape(nb * chunk, HEAD_DIM) for h in range(N_HEADS)], axis=-1)


def _inproj_body(*refs, tm, prompt, n_stack_in, layer):
    x_ref, nw_ref, wab_ref, wcd_ref, we_ref = refs[0:5]
    first_out = 5 + n_stack_in
    pa_ref, pb_ref, pc_ref, pd_ref, pe_ref = refs[first_out:first_out + 5]
    x = x_ref[...]
    ms = jnp.mean(x * x, axis=-1, keepdims=True)
    h = (x * lax.rsqrt(ms + NORM_EPS) * nw_ref[...]).astype(BF16)
    wide = 4 * BRANCH
    pb_ref[...] = jnp.dot(h, wab_ref[:, wide:2 * wide], preferred_element_type=F32)
    pc_ref[...] = jnp.dot(h, wcd_ref[:, 0:wide], preferred_element_type=F32)
    pd_ref[...] = jnp.dot(h, wcd_ref[:, wide:wide + 2 * BRANCH], preferred_element_type=F32)
    pe_ref[...] = jnp.dot(h, we_ref[...], preferred_element_type=F32)
    p_a = jnp.dot(h, wab_ref[:, 0:wide], preferred_element_type=F32)
    pa_ref[...] = p_a
    if not prompt:
        return
    class_refs = refs[first_out + 5:first_out + 5 + len(DILATED)]
    k_keep, v_keep, slab = refs[first_out + 5 + len(DILATED):]
    if n_stack_in:
        k_keep[...] = p_a[:, BRANCH:2 * BRANCH]
        v_keep[...] = p_a[:, 2 * BRANCH:3 * BRANCH]
    else:
        k_keep[...] = jnp.zeros(k_keep.shape, F32)
        v_keep[...] = jnp.zeros(v_keep.shape, F32)
        k_keep[layer] = p_a[:, BRANCH:2 * BRANCH]
        v_keep[layer] = p_a[:, 2 * BRANCH:3 * BRANCH]
    n_slab = 3 * BRANCH // LANES
    for c in range(n_slab):
        slab[c] = p_a[:, c * LANES:(c + 1) * LANES]
    for ref, dil in zip(class_refs, DILATED):
        for r in range(dil):
            for c in range(n_slab):
                ref[0, r, :, c * LANES:(c + 1) * LANES] = slab[c, pl.ds(r, tm // dil, stride=dil), :].astype(BF16)


def _inproj(x2d, nw, w, tm, seq_len=None, keep=None):
    t, d = x2d.shape
    widths = P_WIDTHS
    out_specs = [pl.BlockSpec((tm, wd), lambda i: (i, 0)) for wd in widths]
    out_shape = [jax.ShapeDtypeStruct((t, wd), F32) for wd in widths]
    in_specs = [pl.BlockSpec((tm, d), lambda i: (i, 0)),
                pl.BlockSpec((1, d), lambda i: (0, 0))] + [pl.BlockSpec(wi.shape, lambda i: (0, 0)) for wi in w]
    args = [x2d, nw, *w]
    aliases, scratch = {}, []
    prompt = seq_len is not None
    if prompt:
        tps = seq_len // tm
        for dil in DILATED:
            out_specs.append(pl.BlockSpec((1, dil, tm // dil, 3 * BRANCH), lambda i: (i // tps, 0, i % tps, 0)))
            out_shape.append(jax.ShapeDtypeStruct((t // seq_len, dil, seq_len // dil, 3 * BRANCH), BF16))
        k_stack, v_stack, layer, stack_shape = keep
        skip = tps - stack_shape[2] // tm
        for stack in (k_stack, v_stack):
            if stack is None:
                out_specs.append(pl.BlockSpec((stack_shape[0], None, tm, BRANCH),
                                              lambda i: (0, i // tps, jnp.maximum(i % tps - skip, 0), 0)))
            else:
                aliases[len(args)] = len(out_shape)
                args.append(stack)
                in_specs.append(pl.BlockSpec(memory_space=pl.ANY))
                out_specs.append(pl.BlockSpec((None, None, tm, BRANCH),
                                              lambda i: (layer, i // tps, jnp.maximum(i % tps - skip, 0), 0)))
            out_shape.append(jax.ShapeDtypeStruct(stack_shape, F32))
        scratch = [pltpu.VMEM((3 * BRANCH // LANES, tm, LANES), F32)]
    return pl.pallas_call(
        functools.partial(_inproj_body, tm=tm, prompt=prompt, n_stack_in=len(aliases),
                          layer=keep[2] if prompt else 0),
        grid=(t // tm,),
        in_specs=in_specs,
        out_specs=out_specs,
        out_shape=out_shape,
        input_output_aliases=aliases,
        scratch_shapes=scratch,
        compiler_params=pltpu.CompilerParams(dimension_semantics=("arbitrary",), vmem_limit_bytes=VMEM_LIMIT),
        name="inproj",
    )(*args)


def _swa_body(q_ref, kp_ref, kc_ref, vp_ref, vc_ref, o_ref, lse_ref, *, dil, tq, n_sub):
    n = pl.program_id(1)
    q = (q_ref[...].astype(F32) * (HEAD_DIM ** -0.5)).astype(BF16)
    k = jnp.concatenate([kp_ref[...], kc_ref[...]], axis=0).astype(BF16)
    v = jnp.concatenate([vp_ref[...], vc_ref[...]], axis=0).astype(BF16)
    iq = lax.broadcasted_iota(jnp.int32, (tq, 2 * tq), 0)
    jk = lax.broadcasted_iota(jnp.int32, (tq, 2 * tq), 1)
    dist = iq + tq - jk
    in_band = (dist >= 0) & (dist <= ATTN_STEPS)
    bias = (dist * dil).astype(F32)
    head_bias = [ALIBI_SLOPES[h] * bias for h in range(N_HEADS)]
    first_valid = in_band & ((jk >= tq) | (n > 0))
    ones = jnp.ones((v.shape[0], HEAD_DIM), BF16)
    v_aug = [jnp.concatenate([v[:, h * HEAD_DIM:(h + 1) * HEAD_DIM], ones], axis=-1) for h in range(N_HEADS)]
    for i in range(n_sub):
        valid = first_valid if i == 0 else in_band
        rows = slice(i * tq, (i + 1) * tq)
        keys = slice(i * tq, (i + 2) * tq)
        outs, lses = [], []
        for h in range(N_HEADS):
            sl = slice(h * HEAD_DIM, (h + 1) * HEAD_DIM)
            s = lax.dot_general(q[rows, sl], k[keys, sl], (((1,), (1,)), ((), ())), preferred_element_type=F32)
            s = jnp.where(valid, s - head_bias[h], NEG_MASK)
            m = jnp.max(s, axis=-1, keepdims=True)
            p = jnp.exp(s - m)
            nd = jnp.dot(p.astype(BF16), v_aug[h][keys], preferred_element_type=F32)
            den = nd[:, HEAD_DIM:2 * HEAD_DIM]
            outs.append(nd[:, 0:HEAD_DIM] / den)
            lses.append(m + jnp.log(den))
        o_ref[rows, :] = jnp.concatenate(outs, axis=-1)
        lse_ref[rows, :] = jnp.concatenate(lses, axis=-1)


def _swa(qkv, dil, n_sub):
    b, _, n, _ = qkv.shape
    tq = ATTN_STEPS
    tqb = tq * n_sub

    def cur(col):
        return pl.BlockSpec((None, None, tqb, BRANCH), lambda zi, ni: (zi // dil, zi % dil, ni, col))

    def prev(col):
        return pl.BlockSpec((None, None, tq, BRANCH),
                            lambda zi, ni: (zi // dil, zi % dil, jnp.maximum(ni * n_sub - 1, 0), col))

    shp = jax.ShapeDtypeStruct((b, dil, n, BRANCH), F32)
    return pl.pallas_call(
        functools.partial(_swa_body, dil=dil, tq=tq, n_sub=n_sub),
        grid=(b * dil, n // tqb),
        in_specs=[cur(0), prev(1), cur(1), prev(2), cur(2)],
        out_specs=[cur(0), cur(0)],
        out_shape=[shp, shp],
        compiler_params=pltpu.CompilerParams(dimension_semantics=("arbitrary", "arbitrary"),
                                             vmem_limit_bytes=VMEM_LIMIT),
        name=f"swa_d{dil}",
    )(qkv, qkv, qkv, qkv, qkv)


def _merge_patterns(o_refs, lse_refs, scratch, tm):
    outs, lses = [], []
    si = 0
    for (_, dil), o_ref, lse_ref in zip(ATTN_PATTERNS, o_refs, lse_refs):
        if dil == 1:
            outs.append(o_ref[...])
            lses.append(lse_ref[...])
            continue
        ob, lb = scratch[si], scratch[si + 1]
        si += 2
        n_slab = BRANCH // LANES
        for r in range(dil):
            for c in range(n_slab):
                ob[c, pl.ds(r, tm // dil, stride=dil), :] = o_ref[r, :, c * LANES:(c + 1) * LANES]
                lb[c, pl.ds(r, tm // dil, stride=dil), :] = lse_ref[r, :, c * LANES:(c + 1) * LANES]
        outs.append(jnp.concatenate([ob[c] for c in range(n_slab)], axis=-1))
        lses.append(jnp.concatenate([lb[c] for c in range(n_slab)], axis=-1))
    l_all = lses[0]
    for l in lses[1:]:
        l_all = jnp.maximum(l_all, l)
    numer = jnp.zeros_like(l_all)
    denom = jnp.zeros_like(l_all)
    for o, l in zip(outs, lses):
        wgt = jnp.exp(l - l_all)
        numer = numer + wgt * o
        denom = denom + wgt
    return numer / denom


def _attn_sample_one(pa, kt, vt, t_new, buf):
    rows = N_HEADS * t_new
    q = pa[:, 0:BRANCH] * (HEAD_DIM ** -0.5)
    k_new = pa[:, BRANCH:2 * BRANCH]
    v_new = pa[:, 2 * BRANCH:3 * BRANCH]
    gate = pa[:, 3 * BRANCH:4 * BRANCH]
    row_h = lax.broadcasted_iota(jnp.int32, (rows, BRANCH), 0) // t_new
    lane_h = lax.broadcasted_iota(jnp.int32, (rows, BRANCH), 1) // HEAD_DIM
    own = row_h == lane_h
    q_bd = jnp.where(own, jnp.concatenate([q] * N_HEADS, axis=0), 0.0).astype(BF16)
    pad = jnp.zeros((LANES - t_new, BRANCH), F32)
    k_new_p = jnp.concatenate([k_new, pad], axis=0).astype(BF16)
    v_new_p = jnp.concatenate([v_new, pad], axis=0).astype(BF16)
    kt = kt.astype(BF16)
    vt = vt.astype(BF16)
    nt = (((1,), (1,)), ((), ()))
    s = jnp.concatenate([jnp.dot(q_bd, kt, preferred_element_type=F32),
                         lax.dot_general(q_bd, k_new_p, nt, preferred_element_type=F32)], axis=-1)
    ncol = buf + LANES
    col = lax.broadcasted_iota(jnp.int32, (rows, ncol), 1)
    row = lax.broadcasted_iota(jnp.int32, (rows, ncol), 0)
    delta = buf + row % t_new - col
    hrow = row // t_new
    slope = jnp.where(hrow == 0, ALIBI_SLOPES[0],
                      jnp.where(hrow == 1, ALIBI_SLOPES[1], jnp.where(hrow == 2, ALIBI_SLOPES[2], ALIBI_SLOPES[3])))
    s = s - slope * delta.astype(F32)
    ps, ms, dens = [], [], []
    for win, dil in ATTN_PATTERNS:
        valid = (delta >= 0) & (delta <= win) & ((delta & (dil - 1)) == 0)
        sp = jnp.where(valid, s, NEG_MASK)
        m = jnp.max(sp, axis=-1, keepdims=True)
        p = jnp.exp(sp - m)
        ps.append(p.astype(BF16))
        ms.append(m)
        dens.append(jnp.sum(p, axis=-1, keepdims=True))
    p_all = jnp.concatenate(ps, axis=0)
    num_all = (lax.dot_general(p_all[:, :buf], vt, nt, preferred_element_type=F32)
               + jnp.dot(p_all[:, buf:], v_new_p, preferred_element_type=F32))
    m_all = jnp.maximum(jnp.maximum(ms[0], ms[1]), ms[2])
    numer = jnp.zeros((rows, BRANCH), F32)
    denom = jnp.zeros((rows, 1), F32)
    for g in range(len(ATTN_PATTERNS)):
        wgt = jnp.exp(ms[g] - m_all)
        numer = numer + wgt * num_all[g * rows:(g + 1) * rows]
        denom = denom + wgt * dens[g]
    o_full = jnp.where(own, numer / denom, 0.0)
    o = o_full[0:t_new]
    for h in range(1, N_HEADS):
        o = o + o_full[h * t_new:(h + 1) * t_new]
    return o * _silu(gate)


def _attn_sample_body(pa_ref, kt_ref, vt_ref, out_ref, *, t_new, buf, nb):
    for j in range(nb):
        out_ref[j] = _attn_sample_one(pa_ref[j], kt_ref[j], vt_ref[j], t_new, buf)


def _store_state(s_out_ref, state, create_layer):
    if create_layer is None:
        s_out_ref[...] = state
    else:
        s_out_ref[...] = jnp.zeros(s_out_ref.shape, F32)
        s_out_ref[create_layer] = state


def _delta_body(*refs, ns, tl, chunk, n_tiles, create_layer):
    pb_ref, pe_ref, cpre_ref, s0_ref, cw_ref, alog_ref, dtb_ref, nw_ref = refs[0:8]
    y_ref, s_out_ref, xbuf, s_scr = refs[-4:]
    t = pl.program_id(1)
    nc = tl // chunk
    nb = ns * nc
    nbh = nb * N_HEADS
    rows = ns * tl
    width = 3 * BRANCH

    @pl.when(t == 0)
    def _():
        xbuf[:, 0:SUBLANES, :] = cpre_ref[...]
        s_scr[...] = s0_ref[...]

    xbuf[:, SUBLANES:SUBLANES + tl, :] = pb_ref[:, :, 0:width]
    conv = jnp.zeros((ns, tl, width), F32)
    for tap in range(CONV_TAPS):
        off = SUBLANES - (CONV_TAPS - 1) + tap
        conv = conv + xbuf[:, off:off + tl, :] * cw_ref[tap:tap + 1, :]
    if n_tiles > 1:
        xbuf[:, 0:SUBLANES, :] = xbuf[:, tl:tl + SUBLANES, :]
    conv = _silu(conv).reshape(rows, width)
    g_ones = _head_ones()
    q = conv[:, 0:BRANCH]
    k = conv[:, BRANCH:2 * BRANCH]
    v = conv[:, 2 * BRANCH:3 * BRANCH]
    q = q * lax.rsqrt(_head_sum(q * q, g_ones) + L2_EPS) * (HEAD_DIM ** -0.5)
    k = k * lax.rsqrt(_head_sum(k * k, g_ones) + L2_EPS)
    pe = pe_ref[...].reshape(rows, LANES)
    lane = lax.broadcasted_iota(jnp.int32, (rows, LANES), 1)
    gb = jnp.where(lane < N_HEADS, _sigmoid(pe), -jnp.exp(alog_ref[...]) * _softplus(pe + dtb_ref[...]))

    qh = _to_head_batch(q, nb, chunk)
    kh = _to_head_batch(k, nb, chunk)
    vh = _to_head_batch(v, nb, chunk)

    def col_batch(first):
        cols = [gb[:, first + h:first + h + 1].reshape(nb, 1, chunk, 1) for h in range(N_HEADS)]
        return jnp.concatenate(cols, axis=1).reshape(nbh, chunk, 1)

    beta = col_batch(0)
    g = col_batch(N_HEADS)
    ri = lax.broadcasted_iota(jnp.int32, (nbh, chunk, chunk), 1)
    ci = lax.broadcasted_iota(jnp.int32, (nbh, chunk, chunk), 2)
    ltri = jnp.where(ri >= ci, 1.0, 0.0).astype(BF16)
    eye = jnp.where(ri == ci, 1.0, 0.0).astype(F32)
    gm = jnp.where(ri > ci, jnp.broadcast_to(g, (nbh, chunk, chunk)), 0.0)
    dmat = sum(_bmm(ltri, term) for term in _split3(gm))
    gc = dmat[:, :, 0:1] + g[:, 0:1, :]
    g_last = gc[:, chunk - 1:chunk, :]
    decay = jnp.where(ri >= ci, jnp.exp(dmat), 0.0)
    kb = kh * beta
    vb = vh * beta
    kh_b = kh.astype(BF16)
    a_mat = jnp.where(ri > ci, _bmm_nt(kb.astype(BF16), kh_b) * decay, 0.0)
    n_fac = int(math.log2(chunk))
    x = -a_mat
    u = eye + x
    x = _bmm1(x, x)
    for _ in range(2, n_fac):
        prod = _bmm1(jnp.concatenate([x, u], axis=1), x)
        x = prod[:, 0:chunk]
        u = u + prod[:, chunk:2 * chunk]
    t_inv = u + _bmm1(u, x)
    uw = _bmm3(t_inv, jnp.concatenate([vb, kb * jnp.exp(gc)], axis=-1))
    u_mat = uw[:, :, 0:HEAD_DIM]
    w_mat = uw[:, :, HEAD_DIM:2 * HEAD_DIM]
    qk = _bmm_nt(qh.astype(BF16), kh_b) * decay
    qg = qh * jnp.exp(gc)
    kg = kh * jnp.exp(g_last - gc)
    e_last = jnp.exp(g_last)

    def sel(a, c):
        a5 = a.reshape((ns, nc, N_HEADS) + a.shape[1:])
        return a5[:, c].reshape((ns * N_HEADS,) + a.shape[1:])

    st = s_scr[...].reshape(ns * N_HEADS, HEAD_DIM, HEAD_DIM)
    o_chunks = []
    for c in range(nc):
        s_b = st.astype(BF16)
        wq = _bmm(jnp.concatenate([sel(w_mat, c), sel(qg, c)], axis=1).astype(BF16), s_b)
        v_new = (sel(u_mat, c) - wq[:, 0:chunk]).astype(BF16)
        o_chunks.append(wq[:, chunk:2 * chunk] + _bmm(sel(qk, c).astype(BF16), v_new))
        st = st * sel(e_last, c) + _bmm_tn(sel(kg, c).astype(BF16), v_new)
    s_scr[...] = st.reshape(ns, N_HEADS, HEAD_DIM, HEAD_DIM)
    if nc > 1:
        o_all = jnp.concatenate([oc.reshape(ns, 1, N_HEADS, chunk, HEAD_DIM) for oc in o_chunks], axis=1)
        o_all = o_all.reshape(nbh, chunk, HEAD_DIM)
    else:
        o_all = o_chunks[0]
    o = _from_head_batch(o_all, nb, chunk)
    o = o * lax.rsqrt(_head_sum(o * o, g_ones) * (1.0 / HEAD_DIM) + NORM_EPS) * nw_ref[...]
    gate = pb_ref[:, :, width:width + BRANCH].reshape(rows, BRANCH)
    y_ref[...] = (o * _silu(gate)).reshape(ns, tl, BRANCH)

    @pl.when(t == n_tiles - 1)
    def _():
        _store_state(s_out_ref, s_scr[...], create_layer)


def _state_spec(ns, layer):
    return pl.BlockSpec((None, ns, N_HEADS, HEAD_DIM, HEAD_DIM), lambda bi, ti: (layer, bi, 0, 0, 0))


def _state_out(ns, layer, s_stack, depth):
    if s_stack is not None:
        return _state_spec(ns, layer), [s_stack], None
    spec = pl.BlockSpec((depth, ns, N_HEADS, HEAD_DIM, HEAD_DIM), lambda bi, ti: (0, bi, 0, 0, 0))
    return spec, [], layer


def _delta(pb, pe, cpre, s0, s0_layer, s_stack, layer, depth, cw, alog_row, dtb_row, nw_row, *, ns, tl, chunk):
    b, l, _ = pb.shape
    n_tiles = l // tl
    seq3 = lambda w: pl.BlockSpec((ns, tl, w), lambda bi, ti: (bi, ti, 0))
    const2 = lambda a: pl.BlockSpec(a.shape, lambda bi, ti: (0, 0))
    out_state, alias_args, create_layer = _state_out(ns, layer, s_stack, depth)
    return pl.pallas_call(
        functools.partial(_delta_body, ns=ns, tl=tl, chunk=chunk, n_tiles=n_tiles, create_layer=create_layer),
        grid=(b // ns, n_tiles),
        in_specs=[seq3(4 * BRANCH), seq3(LANES),
                  pl.BlockSpec((ns, SUBLANES, 3 * BRANCH), lambda bi, ti: (bi, 0, 0)), _state_spec(ns, s0_layer),
                  const2(cw), const2(alog_row), const2(dtb_row), const2(nw_row)]
        + [pl.BlockSpec(memory_space=pl.ANY)] * len(alias_args),
        out_specs=[seq3(BRANCH), out_state],
        out_shape=[jax.ShapeDtypeStruct((b, l, BRANCH), F32),
                   jax.ShapeDtypeStruct((depth, b, N_HEADS, HEAD_DIM, HEAD_DIM), F32)],
        input_output_aliases={8: 1} if alias_args else {},
        scratch_shapes=[pltpu.VMEM((ns, SUBLANES + tl, 3 * BRANCH), F32),
                        pltpu.VMEM((ns, N_HEADS, HEAD_DIM, HEAD_DIM), F32)],
        compiler_params=pltpu.CompilerParams(dimension_semantics=("arbitrary", "arbitrary"),
                                             vmem_limit_bytes=VMEM_LIMIT),
        name="delta",
    )(pb, pe, cpre, s0, cw, alog_row, dtb_row, nw_row, *alias_args)


def _hgrn_body(*refs, ns, tl, chunk, n_tiles, layer, create_layer):
    pc_ref, lbraw_ref, s0_ref, nw_ref = refs[0:4]
    y_ref, s_out_ref, s_scr = refs[-3:]
    t = pl.program_id(1)
    nc = tl // chunk
    nb = ns * nc
    rows = ns * tl

    def flip(s4):
        return jnp.swapaxes(s4.reshape(ns * N_HEADS, HEAD_DIM, HEAD_DIM), 1, 2).reshape(s4.shape)

    @pl.when(t == 0)
    def _():
        s_scr[...] = flip(s0_ref[...])

    raw = lbraw_ref[...]
    e = jnp.exp(raw - jnp.max(raw, axis=0, keepdims=True))
    sm = e / jnp.sum(e, axis=0, keepdims=True)
    lb = jnp.zeros((1, BRANCH), F32)
    for d in range(1, layer + 1):
        lb = lb + sm[d:d + 1, :]

    pc = pc_ref[...].reshape(rows, 4 * BRANCH)
    qh = _silu(pc[:, 0:BRANCH])
    fr = pc[:, BRANCH:2 * BRANCH]
    vh = pc[:, 2 * BRANCH:3 * BRANCH]
    gate = pc[:, 3 * BRANCH:4 * BRANCH]
    f = lb + (1.0 - lb) * _sigmoid(fr)
    log_f = jnp.log(f).reshape(nb, chunk, BRANCH)
    kh = (1.0 - lb) * _sigmoid(-fr)

    q3 = qh.reshape(nb, chunk, BRANCH)
    k3 = kh.reshape(nb, chunk, BRANCH)
    v3 = vh.reshape(nb, chunk, BRANCH)
    ri = lax.broadcasted_iota(jnp.int32, (nb, chunk, chunk), 1)
    ci = lax.broadcasted_iota(jnp.int32, (nb, chunk, chunk), 2)
    ltri = jnp.where(ri >= ci, 1.0, 0.0).astype(BF16)
    gcum = sum(_bmm(ltri, term) for term in _split3(log_f))
    g_last = gcum[:, chunk - 1:chunk, :]
    qg = q3 * jnp.exp(gcum)
    kg = k3 * jnp.exp(g_last - gcum)
    e_last = jnp.exp(g_last)

    g_ones = _head_ones()
    n_grp = chunk // SUBLANES
    row_i = [g * SUBLANES + lax.broadcasted_iota(jnp.int32, (nb, chunk - g * SUBLANES, BRANCH), 1)
             for g in range(n_grp)]
    acc = [jnp.zeros((nb, SUBLANES, BRANCH), F32) for _ in range(n_grp)]
    for j in range(chunk):
        g0 = j // SUBLANES
        lo = g0 * SUBLANES
        dec = jnp.exp(jnp.where(row_i[g0] >= j, gcum[:, lo:, :] - gcum[:, j:j + 1, :], NEG_MASK))
        t_j = (q3[:, lo:, :] * dec * k3[:, j:j + 1, :]).reshape(nb * (chunk - lo), BRANCH)
        a_j = jnp.dot(t_j.astype(BF16), g_ones, preferred_element_type=F32).reshape(nb, chunk - lo, BRANCH)
        c_j = a_j * v3[:, j:j + 1, :]
        for gi in range(g0, n_grp):
            acc[gi] = acc[gi] + c_j[:, (gi - g0) * SUBLANES:(gi - g0 + 1) * SUBLANES, :]
    intra = jnp.concatenate(acc, axis=1) if n_grp > 1 else acc[0]

    qg_b = _to_head_batch(qg.reshape(rows, BRANCH), nb, chunk).astype(BF16)
    kg_b = _to_head_batch(kg.reshape(rows, BRANCH), nb, chunk).astype(BF16)
    v_b = _to_head_batch(vh, nb, chunk).astype(BF16)
    el_b = _to_head_batch(e_last.reshape(nb, BRANCH), nb, 1)

    def sel(a, c):
        a5 = a.reshape((ns, nc, N_HEADS) + a.shape[1:])
        return a5[:, c].reshape((ns * N_HEADS,) + a.shape[1:])

    st = s_scr[...].reshape(ns * N_HEADS, HEAD_DIM, HEAD_DIM)
    o_chunks = []
    for c in range(nc):
        o_chunks.append(_bmm_nt(sel(qg_b, c), st.astype(BF16)))
        st = st * sel(el_b, c) + _bmm_tn(sel(v_b, c), sel(kg_b, c))
    s_scr[...] = st.reshape(ns, N_HEADS, HEAD_DIM, HEAD_DIM)
    if nc > 1:
        o_all = jnp.concatenate([oc.reshape(ns, 1, N_HEADS, chunk, HEAD_DIM) for oc in o_chunks], axis=1)
        o_all = o_all.reshape(nb * N_HEADS, chunk, HEAD_DIM)
    else:
        o_all = o_chunks[0]
    o = _from_head_batch(o_all, nb, chunk) + intra.reshape(rows, BRANCH)
    o = o * lax.rsqrt(_head_sum(o * o, g_ones) * (1.0 / HEAD_DIM) + NORM_EPS) * nw_ref[...]
    y_ref[...] = (o * _silu(gate)).reshape(ns, tl, BRANCH)

    @pl.when(t == n_tiles - 1)
    def _():
        _store_state(s_out_ref, flip(s_scr[...]), create_layer)


def _hgrn_attn_body(*refs, hgrn_args, attn_args):
    pa_ref, kt_ref, vt_ref, y_ref, s_out_ref, ya_ref, s_scr = refs[-7:]
    _hgrn_body(*refs[:-7], y_ref, s_out_ref, s_scr, **hgrn_args)
    _attn_sample_body(pa_ref, kt_ref, vt_ref, ya_ref, **attn_args)


def _hgrn(pc, lb_raw, s0, s0_layer, s_stack, depth, nw_row, *, ns, tl, chunk, layer, rider=None):
    b, l, _ = pc.shape
    n_tiles = l // tl
    out_state, alias_args, create_layer = _state_out(ns, layer, s_stack, depth)
    hgrn_args = dict(ns=ns, tl=tl, chunk=chunk, n_tiles=n_tiles, layer=layer, create_layer=create_layer)
    in_specs = [pl.BlockSpec((ns, tl, 4 * BRANCH), lambda bi, ti: (bi, ti, 0)),
                pl.BlockSpec(lb_raw.shape, lambda bi, ti: (0, 0)), _state_spec(ns, s0_layer),
                pl.BlockSpec(nw_row.shape, lambda bi, ti: (0, 0))] + [pl.BlockSpec(memory_space=pl.ANY)] * len(alias_args)
    out_specs = [pl.BlockSpec((ns, tl, BRANCH), lambda bi, ti: (bi, ti, 0)), out_state]
    out_shape = [jax.ShapeDtypeStruct((b, l, BRANCH), F32),
                 jax.ShapeDtypeStruct((depth, b, N_HEADS, HEAD_DIM, HEAD_DIM), F32)]
    args = [pc, lb_raw, s0, nw_row, *alias_args]
    body = functools.partial(_hgrn_body, **hgrn_args)
    if rider is not None:
        pa_s, cache_kt, cache_vt = rider
        assert b == ns and pa_s.shape[0] % n_tiles == 0
        nb = pa_s.shape[0] // n_tiles
        t_new, buf = pa_s.shape[1], cache_kt.shape[3]
        cache = pl.BlockSpec((None, nb, BRANCH, buf), lambda bi, ti: (layer, ti, 0, 0))
        in_specs += [pl.BlockSpec((nb, t_new, 4 * BRANCH), lambda bi, ti: (ti, 0, 0)), cache, cache]
        out_specs.append(pl.BlockSpec((nb, t_new, BRANCH), lambda bi, ti: (ti, 0, 0)))
        out_shape.append(jax.ShapeDtypeStruct((pa_s.shape[0], t_new, BRANCH), F32))
        args += [pa_s, cache_kt, cache_vt]
        body = functools.partial(_hgrn_attn_body, hgrn_args=hgrn_args, attn_args=dict(t_new=t_new, buf=buf, nb=nb))
    return pl.pallas_call(
        body,
        grid=(b // ns, n_tiles),
        in_specs=in_specs,
        out_specs=out_specs,
        out_shape=out_shape,
        input_output_aliases={4: 1} if alias_args else {},
        scratch_shapes=[pltpu.VMEM((ns, N_HEADS, HEAD_DIM, HEAD_DIM), F32)],
        compiler_params=pltpu.CompilerParams(dimension_semantics=("arbitrary", "arbitrary"),
                                             vmem_limit_bytes=VMEM_LIMIT),
        name="hgrn" if rider is None else "hgrn_attn",
    )(*args)


def _post_body(*refs, ns, tl, n_tiles, start_pos, final, n_attn):
    attn_refs = refs[0:n_attn]
    (x_ref, yb_ref, yc_ref, pd_ref, ppre_ref, pw_ref, ps_ref, wo_ref, fw_ref, out_ref, xbuf) = refs[n_attn:n_attn + 11]
    slabs = refs[n_attn + 11:]
    t = pl.program_id(1)
    rows = ns * tl
    if n_attn == 1:
        ya = attn_refs[0][...].reshape(rows, BRANCH)
    else:
        n_g = len(ATTN_PATTERNS)
        ya = _merge_patterns(attn_refs[0:n_g], attn_refs[n_g:2 * n_g], slabs, tl) * _silu(attn_refs[2 * n_g][...])

    @pl.when(t == 0)
    def _():
        xbuf[:, 0:POOL_MAX, :] = ppre_ref[...]

    xbuf[:, POOL_MAX:POOL_MAX + tl, :] = pd_ref[:, :, 0:BRANCH]

    def back(kk):
        return xbuf[:, POOL_MAX - kk:POOL_MAX - kk + tl, :]

    x0 = back(0)
    sums = {}
    acc = x0
    for kk in range(1, POOL_MAX):
        acc = acc + back(kk)
        if kk + 1 in POOL_WINDOWS:
            sums[kk + 1] = acc
    if n_tiles > 1:
        xbuf[:, 0:POOL_MAX, :] = xbuf[:, tl:tl + POOL_MAX, :]
    group = lax.broadcasted_iota(jnp.int32, (ns, tl, BRANCH), 2) // HEAD_DIM
    pos = start_pos + t * tl + lax.broadcasted_iota(jnp.int32, (ns, tl, BRANCH), 1)
    tot = sums[POOL_WINDOWS[-1]]
    win = jnp.full((ns, tl, BRANCH), POOL_WINDOWS[-1], jnp.int32)
    for gi in range(len(POOL_WINDOWS) - 2, -1, -1):
        tot = jnp.where(group == gi, sums[POOL_WINDOWS[gi]], tot)
        win = jnp.where(group == gi, POOL_WINDOWS[gi], win)
    cnt = jnp.minimum(pos + 1, win).astype(F32)
    pooled = (tot / cnt - x0).reshape(rows, BRANCH)
    gate_d = pd_ref[:, :, BRANCH:2 * BRANCH].reshape(rows, BRANCH)
    yd = jnp.dot(pooled.astype(BF16), pw_ref[...], preferred_element_type=F32) * ps_ref[...] * _silu(gate_d)
    ycat = jnp.concatenate([ya, yb_ref[...].reshape(rows, BRANCH),
                            yc_ref[...].reshape(rows, BRANCH), yd], axis=-1).astype(BF16)
    x_new = x_ref[...].reshape(rows, -1) + jnp.dot(ycat, wo_ref[...], preferred_element_type=F32)
    if final:
        ms = jnp.mean(x_new * x_new, axis=-1, keepdims=True)
        x_new = x_new * lax.rsqrt(ms + NORM_EPS) * fw_ref[...]
    out_ref[...] = x_new.reshape(out_ref.shape)


def _post(attn, x, yb, yc, pd, ppre, pw_bd, ps_row, wo, fw_row, *, ns, tl, start_pos, final):
    b, l, d = x.shape
    n_tiles = l // tl
    seq3 = lambda w: pl.BlockSpec((ns, tl, w), lambda bi, ti: (bi, ti, 0))
    const2 = lambda a: pl.BlockSpec(a.shape, lambda bi, ti: (0, 0))
    if len(attn) == 1:
        attn_specs, slabs = [seq3(BRANCH)], []
    else:
        assert ns == 1

        def cls(dil):
            if dil == 1:
                return pl.BlockSpec((None, None, tl, BRANCH), lambda bi, ti: (bi, 0, ti, 0))
            return pl.BlockSpec((None, dil, tl // dil, BRANCH), lambda bi, ti: (bi, 0, ti, 0))

        attn_specs = 2 * [cls(dil) for _, dil in ATTN_PATTERNS] + [
            pl.BlockSpec((None, tl, BRANCH), lambda bi, ti: (bi, ti, 3))]
        slabs = [pltpu.VMEM((BRANCH // LANES, tl, LANES), F32)] * (2 * len(DILATED))
    return pl.pallas_call(
        functools.partial(_post_body, ns=ns, tl=tl, n_tiles=n_tiles, start_pos=start_pos, final=final,
                          n_attn=len(attn)),
        grid=(b // ns, n_tiles),
        in_specs=attn_specs + [seq3(d), seq3(BRANCH), seq3(BRANCH), seq3(2 * BRANCH),
                               pl.BlockSpec((ns, POOL_MAX, BRANCH), lambda bi, ti: (bi, 0, 0)),
                               const2(pw_bd), const2(ps_row), const2(wo), const2(fw_row)],
        out_specs=seq3(d),
        out_shape=jax.ShapeDtypeStruct((b, l, d), F32),
        scratch_shapes=[pltpu.VMEM((ns, POOL_MAX + tl, BRANCH), F32)] + slabs,
        compiler_params=pltpu.CompilerParams(dimension_semantics=("arbitrary", "arbitrary"),
                                             vmem_limit_bytes=VMEM_LIMIT),
        name="post",
    )(*attn, x, yb, yc, pd, ppre, pw_bd, ps_row, wo, fw_row)


def _prep_weights(w_in, w_out, pool_w):
    n_small = 2 * N_HEADS
    main = 8 * BRANCH
    w_perm = (w_in[:, :, 0:main].astype(BF16), w_in[:, :, main + n_small:].astype(BF16),
              jnp.pad(w_in[:, :, main:main + n_small], ((0, 0), (0, 0), (0, LANES - n_small))).astype(BF16))
    depth, groups = pool_w.shape[0:2]
    pw_bd = jnp.zeros((depth, BRANCH, BRANCH), pool_w.dtype)
    for g in range(groups):
        sl = slice(g * HEAD_DIM, (g + 1) * HEAD_DIM)
        pw_bd = pw_bd.at[:, sl, sl].set(pool_w[:, g])
    return w_perm, w_out.astype(BF16), pw_bd.astype(BF16)


def _attn_prompt(pa3, classes, max_sub):
    b, l, w = pa3.shape
    os_, lses = [], []
    ci = 0
    for _, dil in ATTN_PATTERNS:
        n_sub = min(max_sub, l // dil // ATTN_STEPS)
        if dil == 1:
            o, lse = _swa(pa3.reshape(b, 1, l, w), 1, n_sub)
        else:
            o, lse = _swa(classes[ci], dil, n_sub)
            ci += 1
        os_.append(o)
        lses.append(lse)
    return os_ + lses + [pa3]


def _trunk(x, start_pos, states, prm, cfg):
    b, l, d = x.shape
    assert l % cfg['tl'] == 0 and l % cfg['tl_hgrn'] == 0 and l % cfg['tl_post'] == 0 and (b * l) % cfg['tm'] == 0
    depth = prm['norm_w'].shape[0]
    dt = x.dtype
    ns, tl = cfg['ns'], cfg['tl']
    prompt = states is None
    state_shape = (depth, b, N_HEADS, HEAD_DIM, HEAD_DIM)
    d_stack = h_stack = k_stack = v_stack = None
    if prompt:
        keep = min(ATTN_PATTERNS[-1][0], l)
        kv_shape = (depth, b, keep, BRANCH)
        d_in = h_in = jnp.zeros((1,) + state_shape[1:], F32)
    else:
        c_kt, c_vt, d_in, c_conv, h_in, c_pool = states
    ks, vs, dcs, ps = [], [], [], []
    for layer in range(depth):
        if prompt:
            conv_prefix = jnp.zeros((b, CONV_TAPS - 1, 3 * BRANCH), dt)
            pool_prefix = jnp.zeros((b, POOL_MAX - 1, BRANCH), dt)
        else:
            conv_prefix, pool_prefix = c_conv[layer], c_pool[layer]
        outs = _inproj(x.reshape(b * l, d), prm['norm_w'][layer][None, :], tuple(wi[layer] for wi in prm['w_in']), cfg['tm'],
                       seq_len=l if prompt else None, keep=(k_stack, v_stack, layer, kv_shape) if prompt else None)
        pa, pb, pc, pd, pe = outs[0:5]
        pa3 = pa.reshape(b, l, 4 * BRANCH)
        s0_layer = 0 if prompt else layer
        hgrn_call = functools.partial(
            _hgrn, pc.reshape(b, l, 4 * BRANCH), prm['hgrn_lb_raw'], h_in, s0_layer, h_stack, depth,
            prm['hgrn_nw_row'][layer], ns=ns, tl=cfg['tl_hgrn'], chunk=math.gcd(l, HGRN_CHUNK), layer=layer)
        if prompt:
            n_cls = len(DILATED)
            attn = _attn_prompt(pa3, outs[5:5 + n_cls], cfg['swa_sub'])
            k_stack, v_stack = outs[5 + n_cls:]
            yc, h_stack = yield hgrn_call
        else:
            attn = [(yield pa3)]
            ks.append(pa3[:, :, BRANCH:2 * BRANCH].reshape(b, l, N_HEADS, HEAD_DIM))
            vs.append(pa3[:, :, 2 * BRANCH:3 * BRANCH].reshape(b, l, N_HEADS, HEAD_DIM))
            yc, h_stack = hgrn_call()

        pb3 = pb.reshape(b, l, 4 * BRANCH)
        cpre = jnp.pad(conv_prefix, ((0, 0), (SUBLANES - (CONV_TAPS - 1), 0), (0, 0)))
        yb, d_stack = _delta(pb3, pe.reshape(b, l, LANES), cpre, d_in, s0_layer, d_stack, layer, depth,
                             prm['delta_conv_w'][layer], prm['alog_row'][layer], prm['dtb_row'][layer],
                             prm['delta_nw_row'][layer], ns=ns, tl=tl, chunk=math.gcd(l, DELTA_CHUNK))
        dcs.append(jnp.concatenate([conv_prefix, pb3[:, :, 0:3 * BRANCH]], axis=1)[:, -(CONV_TAPS - 1):])

        pd3 = pd.reshape(b, l, 2 * BRANCH)
        ppre = jnp.pad(pool_prefix, ((0, 0), (1, 0), (0, 0)))
        ps.append(jnp.concatenate([pool_prefix, pd3[:, :, 0:BRANCH]], axis=1)[:, -(POOL_MAX - 1):])
        x = _post(attn, x, yb, yc, pd3, ppre, prm['pool_w_bd'][layer], prm['pool_scale'][layer][None, :],
                  prm['w_out'][layer], prm['final_norm_w'][None, :], ns=cfg['ns_post'], tl=cfg['tl_post'],
                  start_pos=start_pos, final=(layer == depth - 1))
    if prompt:
        k_all = k_stack.reshape(depth, b, keep, N_HEADS, HEAD_DIM)
        v_all = v_stack.reshape(depth, b, keep, N_HEADS, HEAD_DIM)
    else:
        k_all, v_all = jnp.stack(ks), jnp.stack(vs)
    return x, (k_all, v_all, d_stack, jnp.stack(dcs), h_stack, jnp.stack(ps))


def _lane_row(vals, offset):
    depth, n = vals.shape
    return jnp.zeros((depth, 1, LANES), F32).at[:, 0, offset:offset + n].set(vals.astype(F32))


def kernel(x_prompt, x_sample, cache_attn_k, cache_attn_v, state_delta, state_delta_conv, state_hgrn, state_pool,
           norm_w, w_in, w_out, delta_conv_w, delta_a_log, delta_dt_bias, delta_norm_w, hgrn_lb_raw, hgrn_norm_w,
           pool_w, pool_scale, final_norm_w):
    w_in_p, w_out_b, pw_bd = _prep_weights(w_in, w_out, pool_w)
    prm = dict(
        norm_w=norm_w, w_in=w_in_p, w_out=w_out_b, pool_w_bd=pw_bd, pool_scale=pool_scale,
        final_norm_w=final_norm_w, delta_conv_w=delta_conv_w, hgrn_lb_raw=hgrn_lb_raw,
        alog_row=_lane_row(delta_a_log, N_HEADS), dtb_row=_lane_row(delta_dt_bias, N_HEADS),
        delta_nw_row=jnp.tile(delta_norm_w, (1, N_HEADS))[:, None, :],
        hgrn_nw_row=jnp.tile(hgrn_norm_w, (1, N_HEADS))[:, None, :],
    )
    bp = x_prompt.shape[0]
    dec_b, dec_l = x_sample.shape[0], x_sample.shape[1]
    depth, _, buf = cache_attn_k.shape[0:3]
    ckt = jnp.transpose(cache_attn_k, (0, 1, 3, 4, 2)).reshape(depth, dec_b, BRANCH, buf)
    cvt = jnp.transpose(cache_attn_v, (0, 1, 3, 4, 2)).reshape(depth, dec_b, BRANCH, buf)
    run_p = _trunk(x_prompt, 0, None, prm,
                   dict(tm=512, ns=bp, tl=256, tl_hgrn=128, ns_post=1, tl_post=1024, swa_sub=4))
    run_s = _trunk(x_sample, PAST_LEN, (ckt, cvt, state_delta, state_delta_conv, state_hgrn, state_pool), prm,
                   dict(tm=256, ns=32, tl=dec_l, tl_hgrn=dec_l, ns_post=32, tl_post=dec_l))
    hgrn_call, pa_s = next(run_p), next(run_s)
    for _ in range(depth):
        yc, h_stack, ya_s = hgrn_call(rider=(pa_s, ckt, cvt))
        try:
            hgrn_call = run_p.send((yc, h_stack))
        except StopIteration as done:
            y_p, st_p = done.value
        try:
            pa_s = run_s.send(ya_s)
        except StopIteration as done:
            y_s, st_s = done.value
    k_p, v_p, d_p, dc_p, h_p, pl_p = st_p
    k_s, v_s, d_s, dc_s, h_s, pl_s = st_s
    return (y_p, y_s, k_p, k_s, v_p, v_s, d_p, d_s, dc_p, dc_s, h_p, h_s, pl_p, pl_s)
```
